```python
import math
import jax, jax.numpy as jnp
from jax import lax
import numpy as np

D_MODEL = 1024
BATCH = 2
SEQ = 8192
DEPTH = 2
DEC_BATCH = 128
DEC_SEQ = 1
PAST_LEN = 2048
PAGE_SIZE = 128

HEAD_DIM = 64
SB_HEADS = D_MODEL // HEAD_DIM
DF_HEADS = D_MODEL // (2 * HEAD_DIM)
D_FF = 4 * D_MODEL
REL_BUCKETS = 32
REL_MAX_DIST = 128
Q_BLOCK = 128
N_SB = (DEPTH + 1) // 2
N_DF = DEPTH // 2
ALPHA = (2 * DEPTH) ** 0.25
BETA = (8 * DEPTH) ** -0.25
ATT_SCALE = HEAD_DIM ** -0.5
NEG_BIG = -1e30

kernel_name = "stickbreak_diffattn_hybrid_step"


def _layernorm(x, g, b, eps=1e-5):
    xf = x.astype(jnp.float32)
    mu = jnp.mean(xf, axis=-1, keepdims=True)
    var = jnp.mean(jnp.square(xf - mu), axis=-1, keepdims=True)
    return ((xf - mu) * lax.rsqrt(var + eps) * g + b).astype(x.dtype)


def _mlp(x, w_up, w_down):
    h = jax.nn.relu(x @ w_up)
    return (h * h) @ w_down


def _proj_sb(x, w_in):
    qkv = x @ w_in
    q, k, v = jnp.split(qkv, 3, axis=-1)
    shp = x.shape[:-1] + (SB_HEADS, HEAD_DIM)
    return q.reshape(shp), k.reshape(shp), v.reshape(shp)


def _proj_df(x, w_in):
    qkv = x @ w_in
    q, k, v = jnp.split(qkv, 3, axis=-1)
    shp = x.shape[:-1] + (DF_HEADS, 2, HEAD_DIM)
    return q.reshape(shp), k.reshape(shp), v.reshape(x.shape[:-1] + (DF_HEADS, 2 * HEAD_DIM))


def _t5_bucket(dist):
    n = jnp.maximum(dist, 0)
    max_exact = REL_BUCKETS // 2
    nf = jnp.maximum(n, 1).astype(jnp.float32)
    large = max_exact + (jnp.log(nf / max_exact) / math.log(REL_MAX_DIST / max_exact)
                         * (REL_BUCKETS - max_exact)).astype(jnp.int32)
    large = jnp.minimum(large, REL_BUCKETS - 1)
    return jnp.where(n < max_exact, n, large)


def _sb_block(q, k, v, q_pos, k_pos):
    z = jnp.einsum('bqhd,bkhd->bhqk', q.astype(jnp.float32), k.astype(jnp.float32)) * ATT_SCALE
    mask = (k_pos[None, :] < q_pos[:, None])[None, None]
    log_stay = jnp.where(mask, jax.nn.log_sigmoid(-z), 0.0)
    log_stay_after = lax.cumsum(log_stay, axis=3, reverse=True) - log_stay
    a = jnp.where(mask, jnp.exp(jax.nn.log_sigmoid(z) + log_stay_after), 0.0)
    return jnp.einsum('bhqk,bkhd->bqhd', a, v.astype(jnp.float32))


def _df_block(q, k, v, q_pos, k_pos, lam, rel_bias, subln_g, lambda_init):
    s = jnp.einsum('bqhcd,bkhcd->bhcqk', q.astype(jnp.float32), k.astype(jnp.float32)) * ATT_SCALE
    bias = rel_bias.astype(jnp.float32)[_t5_bucket(q_pos[:, None] - k_pos[None, :])]
    s = s + jnp.transpose(bias, (2, 0, 1))[None, :, None]
    mask = (k_pos[None, :] <= q_pos[:, None])[None, None, None]
    p = jax.nn.softmax(jnp.where(mask, s, NEG_BIG), axis=-1)
    a = p[:, :, 0] - lam * p[:, :, 1]
    o = jnp.einsum('bhqk,bkhe->bqhe', a, v.astype(jnp.float32))
    o = o * lax.rsqrt(jnp.mean(o * o, axis=-1, keepdims=True) + 1e-5)
    return o * subln_g.astype(jnp.float32) * (1.0 - lambda_init)


def _sweep(block_fn, q, pos):
    b, s = q.shape[:2]
    nb = s // Q_BLOCK
    qb = jnp.moveaxis(q.reshape((b, nb, Q_BLOCK) + q.shape[2:]), 1, 0)
    pb = pos.reshape(nb, Q_BLOCK)
    ob = lax.map(lambda a: block_fn(a[0], a[1]), (qb, pb))
    return jnp.moveaxis(ob, 0, 1).reshape((b, s) + ob.shape[3:])


def setup_inputs(seed: int = 0) -> dict:
    key = jax.random.key(seed)
    ks = jax.random.split(key, 20)
    n_pages = PAST_LEN // PAGE_SIZE
    n_pool = (DEC_BATCH * n_pages * 5) // 4
    f32 = jnp.float32
    nrm = lambda k, shp, s: jax.random.normal(k, shp, f32) * s
    perm = jax.random.permutation(ks[6], n_pool)[: DEC_BATCH * n_pages]
    return {
        "x_prompt": nrm(ks[0], (BATCH, SEQ, D_MODEL), 1.0),
        "x_sample": nrm(ks[1], (DEC_BATCH, DEC_SEQ, D_MODEL), 1.0),
        "cache_sb_k": nrm(ks[2], (N_SB, n_pool, PAGE_SIZE, SB_HEADS, HEAD_DIM), 1.0),
        "cache_sb_v": nrm(ks[3], (N_SB, n_pool, PAGE_SIZE, SB_HEADS, HEAD_DIM), 1.0),
        "cache_df_k": nrm(ks[4], (N_DF, n_pool, PAGE_SIZE, DF_HEADS, 2, HEAD_DIM), 1.0),
        "cache_df_v": nrm(ks[5], (N_DF, n_pool, PAGE_SIZE, DF_HEADS, 2 * HEAD_DIM), 1.0),
        "page_table": perm.reshape(DEC_BATCH, n_pages).astype(jnp.int32),
        "w_in": nrm(ks[7], (DEPTH, D_MODEL, 3 * D_MODEL), D_MODEL ** -0.5),
        "w_out": nrm(ks[8], (DEPTH, D_MODEL, D_MODEL), BETA * D_MODEL ** -0.5),
        "ln_g": 1.0 + nrm(ks[9], (DEPTH, 2, D_MODEL), 0.02),
        "ln_b": nrm(ks[10], (DEPTH, 2, D_MODEL), 0.02),
        "df_lambda": nrm(ks[11], (N_DF, 4, HEAD_DIM), 0.1),
        "df_subln_g": 1.0 + nrm(ks[12], (N_DF, 2 * HEAD_DIM), 0.02),
        "rel_bias": nrm(ks[13], (REL_BUCKETS, DF_HEADS), 0.5),
        "w_up": nrm(ks[14], (DEPTH, D_MODEL, D_FF), D_MODEL ** -0.5),
        "w_down": nrm(ks[15], (DEPTH, D_FF, D_MODEL), BETA * D_FF ** -0.5),
    }


def reference(x_prompt, x_sample, cache_sb_k, cache_sb_v, cache_df_k, cache_df_v, page_table,
              w_in, w_out, ln_g, ln_b, df_lambda, df_subln_g, rel_bias, w_up, w_down):
    db, ds = x_sample.shape[:2]
    past = page_table.shape[1] * PAGE_SIZE
    pos_p = jnp.arange(x_prompt.shape[1], dtype=jnp.int32)
    q_pos_s = past + jnp.arange(ds, dtype=jnp.int32)
    k_pos_s = jnp.arange(past + ds, dtype=jnp.int32)
    xp, xs = x_prompt, x_sample
    sb_kp, sb_vp, sb_ks, sb_vs = [], [], [], []
    df_kp, df_vp, df_ks, df_vs = [], [], [], []
    for i in range(DEPTH):
        j = i // 2
        if i % 2 == 0:
            qp, kp, vp = _proj_sb(xp, w_in[i])
            op = _sweep(lambda qb, pb: _sb_block(qb, kp, vp, pb, pos_p), qp, pos_p)
            qs, ks_, vs_ = _proj_sb(xs, w_in[i])
            k_all = jnp.concatenate([cache_sb_k[j, page_table].reshape(db, past, SB_HEADS, HEAD_DIM), ks_], axis=1)
            v_all = jnp.concatenate([cache_sb_v[j, page_table].reshape(db, past, SB_HEADS, HEAD_DIM), vs_], axis=1)
            os_ = _sb_block(qs, k_all, v_all, q_pos_s, k_pos_s)
            sb_kp.append(kp); sb_vp.append(vp); sb_ks.append(ks_); sb_vs.append(vs_)
        else:
            lam_init = 0.8 - 0.6 * math.exp(-0.3 * i)
            lv = df_lambda[j].astype(jnp.float32)
            lam = jnp.exp(jnp.sum(lv[0] * lv[1])) - jnp.exp(jnp.sum(lv[2] * lv[3])) + lam_init
            qp, kp, vp = _proj_df(xp, w_in[i])
            op = _sweep(lambda qb, pb: _df_block(qb, kp, vp, pb, pos_p, lam, rel_bias, df_subln_g[j], lam_init),
                        qp, pos_p)
            qs, ks_, vs_ = _proj_df(xs, w_in[i])
            k_all = jnp.concatenate([cache_df_k[j, page_table].reshape(db, past, DF_HEADS, 2, HEAD_DIM), ks_], axis=1)
            v_all = jnp.concatenate([cache_df_v[j, page_table].reshape(db, past, DF_HEADS, 2 * HEAD_DIM), vs_], axis=1)
            os_ = _df_block(qs, k_all, v_all, q_pos_s, k_pos_s, lam, rel_bias, df_subln_g[j], lam_init)
            df_kp.append(kp); df_vp.append(vp); df_ks.append(ks_); df_vs.append(vs_)
        mp = op.reshape(xp.shape).astype(xp.dtype) @ w_out[i]
        ms = os_.reshape(xs.shape).astype(xs.dtype) @ w_out[i]
        xp = _layernorm(ALPHA * xp + mp, ln_g[i, 0], ln_b[i, 0])
        xs = _layernorm(ALPHA * xs + ms, ln_g[i, 0], ln_b[i, 0])
        xp = _layernorm(ALPHA * xp + _mlp(xp, w_up[i], w_down[i]), ln_g[i, 1], ln_b[i, 1])
        xs = _layernorm(ALPHA * xs + _mlp(xs, w_up[i], w_down[i]), ln_g[i, 1], ln_b[i, 1])
    new_sb_k_prompt = jnp.stack(sb_kp)
    new_sb_v_prompt = jnp.stack(sb_vp)
    new_sb_k_sample = jnp.stack(sb_ks)
    new_sb_v_sample = jnp.stack(sb_vs)
    new_df_k_prompt = jnp.stack(df_kp)
    new_df_v_prompt = jnp.stack(df_vp)
    new_df_k_sample = jnp.stack(df_ks)
    new_df_v_sample = jnp.stack(df_vs)
    return (xp, xs, new_sb_k_prompt, new_sb_v_prompt, new_sb_k_sample, new_sb_v_sample,
            new_df_k_prompt, new_df_v_prompt, new_df_k_sample, new_df_v_sample)
```

```python
import functools
import math

import numpy as np
import jax
import jax.numpy as jnp
from jax import lax
from jax.experimental import pallas as pl
from jax.experimental.pallas import tpu as pltpu

HEAD_DIM = 64
PAGE_SIZE = 128
REL_BUCKETS = 32
REL_MAX_DIST = 128
ATT_SCALE = HEAD_DIM ** -0.5
NEG_BIG = -1e30
LN_EPS = 1e-5
LOG2E = math.log2(math.e)

LANES = 128
MXU_DIM = 256
VMEM_LIMIT_BYTES = 48 * 1024 * 1024

ATT_BLOCK = MXU_DIM
ROW_BLOCK = 512
FF_CHUNK = 1024
DEC_HEADS = 16
DEC_PAGES_PER_STEP = 8

F32 = jnp.float32
BF16 = jnp.bfloat16


def _params(*semantics):
    return pltpu.CompilerParams(dimension_semantics=semantics,
                                vmem_limit_bytes=VMEM_LIMIT_BYTES)


def _dot(a, b):
    return jnp.dot(a, b, preferred_element_type=F32)


def _dot_nt(a, b):
    return lax.dot_general(a, b, (((1,), (1,)), ((), ())), preferred_element_type=F32)


def _suffix_matrix(n):
    r = lax.broadcasted_iota(jnp.int32, (n, n), 0)
    c = lax.broadcasted_iota(jnp.int32, (n, n), 1)
    return jnp.where(r > c, -1.0, 0.0).astype(BF16)


def _split_halves(q2):
    qf = q2.astype(F32)
    first = lax.broadcasted_iota(jnp.int32, qf.shape, 1) < HEAD_DIM
    return jnp.where(first, qf, 0.0).astype(BF16), jnp.where(first, 0.0, qf).astype(BF16)


def _bf16_round(x):
    return x.astype(BF16).astype(F32)


def _softplus2(z):
    return jnp.maximum(z, 0.0) + jnp.log2(1.0 + jnp.exp2(-jnp.abs(z)))


def _proj_kernel(x_ref, w_ref, q_ref, k_ref, v_ref, kb_ref, vb_ref, *, q_scale):
    d = x_ref.shape[1]
    x = x_ref[...].astype(BF16)
    q = _dot(x, w_ref[:, 0:d])
    q_ref[...] = (q * q_scale).astype(BF16)
    k = _dot(x, w_ref[:, d:2 * d])
    k_ref[...] = k
    kb_ref[...] = k.astype(BF16)
    v = _dot(x, w_ref[:, 2 * d:3 * d])
    v_ref[...] = v
    vb_ref[...] = v.astype(BF16)


def _qkv_proj(x, w_bf16, q_scale):
    m, d = x.shape
    bm = min(ROW_BLOCK, m)
    row = pl.BlockSpec((bm, d), lambda i: (i, 0))
    return pl.pallas_call(
        functools.partial(_proj_kernel, q_scale=q_scale),
        grid=(m // bm,),
        in_specs=[row, pl.BlockSpec((d, 3 * d), lambda i: (0, 0))],
        out_specs=[row, row, row, row, row],
        out_shape=[jax.ShapeDtypeStruct((m, d), BF16),
                   jax.ShapeDtypeStruct((m, d), F32),
                   jax.ShapeDtypeStruct((m, d), F32),
                   jax.ShapeDtypeStruct((m, d), BF16),
                   jax.ShapeDtypeStruct((m, d), BF16)],
        compiler_params=_params("parallel"),
        name="qkv_proj",
    )(x, w_bf16)


def _layernorm(xf, g, b):
    mu = jnp.mean(xf, axis=-1, keepdims=True)
    xc = xf - mu
    var = jnp.mean(xc * xc, axis=-1, keepdims=True)
    return xc * lax.rsqrt(var + LN_EPS) * g + b


def _outproj_ln_kernel(o_ref, x_ref, w_ref, g_ref, b_ref, y_ref, *, alpha):
    m = _dot(o_ref[...], w_ref[...])
    y_ref[...] = _layernorm(alpha * x_ref[...] + m, g_ref[...], b_ref[...])


def _outproj_ln(o_bf16, x, w_bf16, g, b, alpha):
    m, d = x.shape
    bm = min(ROW_BLOCK, m)
    row = pl.BlockSpec((bm, d), lambda i: (i, 0))
    vec = pl.BlockSpec((1, d), lambda i: (0, 0))
    return pl.pallas_call(
        functools.partial(_outproj_ln_kernel, alpha=alpha),
        grid=(m // bm,),
        in_specs=[row, row, pl.BlockSpec((d, d), lambda i: (0, 0)), vec, vec],
        out_specs=row,
        out_shape=jax.ShapeDtypeStruct((m, d), F32),
        compiler_params=_params("parallel"),
        name="outproj_ln",
    )(o_bf16, x, w_bf16, g.reshape(1, d), b.reshape(1, d))


def _mlp_ln_kernel(x_ref, wu_ref, wd_ref, g_ref, b_ref, y_ref, acc_ref, *, alpha):
    c = pl.program_id(1)
    x = x_ref[...]
    h = jnp.maximum(_dot(x.astype(BF16), wu_ref[...]), 0.0)
    part = _dot((h * h).astype(BF16), wd_ref[...])

    @pl.when(c == 0)
    def _():
        acc_ref[...] = part

    @pl.when(c > 0)
    def _():
        acc_ref[...] += part

    @pl.when(c == pl.num_programs(1) - 1)
    def _():
        y_ref[...] = _layernorm(alpha * x + acc_ref[...], g_ref[...], b_ref[...])


def _mlp_ln(x, wu_bf16, wd_bf16, g, b, alpha):
    m, d = x.shape
    dff = wu_bf16.shape[1]
    bm = min(ROW_BLOCK, m)
    fc = min(FF_CHUNK, dff)
    row = pl.BlockSpec((bm, d), lambda i, c: (i, 0))
    vec = pl.BlockSpec((1, d), lambda i, c: (0, 0))
    return pl.pallas_call(
        functools.partial(_mlp_ln_kernel, alpha=alpha),
        grid=(m // bm, dff // fc),
        in_specs=[row,
                  pl.BlockSpec((d, fc), lambda i, c: (0, c)),
                  pl.BlockSpec((fc, d), lambda i, c: (c, 0)),
                  vec, vec],
        out_specs=row,
        out_shape=jax.ShapeDtypeStruct((m, d), F32),
        scratch_shapes=[pltpu.VMEM((bm, d), F32)],
        compiler_params=_params("parallel", "arbitrary"),
        name="mlp_ln",
    )(x, wu_bf16, wd_bf16, g.reshape(1, d), b.reshape(1, d))


def _sb_tile(qh, kb, vb, u, mask):
    z = _dot_nt(qh, kb)
    sp = _softplus2(z)
    nls = sp if mask is None else jnp.where(mask, sp, 0.0)
    after = _dot(nls.astype(BF16), u)
    a = jnp.exp2(z - sp + after)
    if mask is not None:
        a = jnp.where(mask, a, 0.0)
    o = _dot(a.astype(BF16), vb)
    total = after[:, 0:1] - _bf16_round(nls[:, 0:1])
    return o, total


def _sb_prompt_kernel(q_ref, k_ref, v_ref, o_ref):
    t = q_ref.shape[1]
    i = pl.program_id(2)
    qa, qb = _split_halves(q_ref[0])
    u = _suffix_matrix(t)
    r = lax.broadcasted_iota(jnp.int32, (t, t), 0)
    c = lax.broadcasted_iota(jnp.int32, (t, t), 1)
    diag_mask = c < r

    def tile(j, mask):
        start = pl.multiple_of(j * t, t)
        kb = k_ref[0, pl.ds(start, t), :]
        vb = v_ref[0, pl.ds(start, t), :]
        oa, ta = _sb_tile(qa, kb, vb, u, mask)
        ob, tb = _sb_tile(qb, kb, vb, u, mask)
        return oa, ta, ob, tb

    acc_a, ca, acc_b, cb = tile(i, diag_mask)

    def body(s, carry):
        acc_a, ca, acc_b, cb = carry
        oa, ta, ob, tb = tile(i - 1 - s, None)
        acc_a = acc_a + jnp.exp2(ca) * oa
        acc_b = acc_b + jnp.exp2(cb) * ob
        return acc_a, ca + ta, acc_b, cb + tb

    acc_a, ca, acc_b, cb = lax.fori_loop(0, i, body, (acc_a, ca, acc_b, cb))
    lane_o = lax.broadcasted_iota(jnp.int32, acc_a.shape, 1)
    o_ref[0] = jnp.where(lane_o < HEAD_DIM, acc_a, acc_b).astype(o_ref.dtype)


def _sb_prompt_attention(q, k, v, batch, seq):
    d = q.shape[1]
    t = min(ATT_BLOCK, seq)
    q3, k3, v3 = (a.reshape(batch, seq, d) for a in (q, k, v))
    kv_spec = pl.BlockSpec((1, seq, LANES), lambda b, h, i: (b, 0, h))
    q_spec = pl.BlockSpec((1, t, LANES), lambda b, h, i: (b, i, h))
    out = pl.pallas_call(
        _sb_prompt_kernel,
        grid=(batch, d // LANES, seq // t),
        in_specs=[q_spec, kv_spec, kv_spec],
        out_specs=q_spec,
        out_shape=jax.ShapeDtypeStruct((batch, seq, d), BF16),
        compiler_params=_params("parallel", "parallel", "arbitrary"),
        name="sb_prompt",
    )(q3, k3, v3)
    return out.reshape(batch * seq, d)


def _t5_bucket_np(dist):
    n = np.maximum(dist, 0).astype(np.int32)
    max_exact = REL_BUCKETS // 2
    nf = np.maximum(n, 1).astype(np.float32)
    large = max_exact + (np.log(nf / np.float32(max_exact)) / np.float32(math.log(REL_MAX_DIST / max_exact))
                         * np.float32(REL_BUCKETS - max_exact)).astype(np.int32)
    large = np.minimum(large, REL_BUCKETS - 1)
    return np.where(n < max_exact, n, large).astype(np.int32)


def _bias_table_kernel(rel_ref, bsub_ref, bdiag_ref, bdec_ref, tab_ref, dec_ref, new_ref):
    h = pl.program_id(0)
    far = rel_ref[REL_BUCKETS - 1, h]

    def gather(bucket):
        out = jnp.zeros(bucket.shape, F32)
        for b in range(REL_BUCKETS):
            out = jnp.where(bucket == b, rel_ref[b, h], out)
        return (out - far) * LOG2E

    tab_ref[0, 0] = gather(bsub_ref[...])
    tab_ref[0, 1] = gather(bdiag_ref[...])
    dec_ref[0] = gather(bdec_ref[...])
    new_ref[0] = jnp.zeros(new_ref.shape[1:], F32) + (rel_ref[0, h] - far) * LOG2E


def _bias_tables(rel_bias, t):
    assert t >= REL_MAX_DIST and PAGE_SIZE >= REL_MAX_DIST
    n_heads = rel_bias.shape[1]
    r = np.arange(t)[:, None]
    c = np.arange(t)[None, :]
    b_sub = _t5_bucket_np(r - c + t)
    b_diag = _t5_bucket_np(r - c)
    b_dec = _t5_bucket_np(PAGE_SIZE - np.arange(PAGE_SIZE))[None, :]
    full = lambda shape: pl.BlockSpec(shape, lambda h: (0,) * len(shape))
    return pl.pallas_call(
        _bias_table_kernel,
        grid=(n_heads,),
        in_specs=[pl.BlockSpec(memory_space=pltpu.SMEM), full((t, t)), full((t, t)), full((1, PAGE_SIZE))],
        out_specs=[pl.BlockSpec((1, 2, t, t), lambda h: (h, 0, 0, 0)),
                   pl.BlockSpec((1, 1, PAGE_SIZE), lambda h: (h, 0, 0)),
                   pl.BlockSpec((1, 1, LANES), lambda h: (h, 0, 0))],
        out_shape=[jax.ShapeDtypeStruct((n_heads, 2, t, t), F32),
                   jax.ShapeDtypeStruct((n_heads, 1, PAGE_SIZE), F32),
                   jax.ShapeDtypeStruct((n_heads, 1, LANES), F32)],
        compiler_params=_params("arbitrary"),
        name="t5_bias_tables",
    )(rel_bias, jnp.asarray(b_sub), jnp.asarray(b_diag), jnp.asarray(b_dec))


def _df_lambda(lv_ref, lam_init):
    lv = lv_ref[...]
    a = jnp.sum(lv[0:1, :] * lv[1:2, :], axis=1, keepdims=True)
    b = jnp.sum(lv[2:3, :] * lv[3:4, :], axis=1, keepdims=True)
    return jnp.exp(a) - jnp.exp(b) + lam_init


def _df_prompt_kernel(lv_ref, g_ref, q_ref, k_ref, v_ref, tab_ref, o_ref, *, lam_init):
    t = q_ref.shape[1]
    i = pl.program_id(2)
    qs = _split_halves(q_ref[0])
    r = lax.broadcasted_iota(jnp.int32, (t, t), 0)
    c = lax.broadcasted_iota(jnp.int32, (t, t), 1)
    causal = c <= r

    def kv(j):
        start = pl.multiple_of(j * t, t)
        return k_ref[0, pl.ds(start, t), :], v_ref[0, pl.ds(start, t), :]

    def scores(qc, kb, bias, mask):
        s = _dot_nt(qc, kb)
        if bias is not None:
            s = s + bias
        if mask is not None:
            s = jnp.where(mask, s, NEG_BIG)
        return s

    def update(state, s, vb):
        m, l, acc = state
        m_new = jnp.maximum(m, jnp.max(s, axis=1, keepdims=True))
        alpha = jnp.exp2(m - m_new)
        p = jnp.exp2(s - m_new)
        l = alpha * l + jnp.sum(p, axis=1, keepdims=True)
        acc = alpha * acc + _dot(p.astype(BF16), vb)
        return m_new, l, acc

    kb, vb = kv(i)
    states = []
    for qc in qs:
        s = scores(qc, kb, tab_ref[0, 1], causal)
        m = jnp.max(s, axis=1, keepdims=True)
        p = jnp.exp2(s - m)
        states.append((m, jnp.sum(p, axis=1, keepdims=True), _dot(p.astype(BF16), vb)))

    def sub_tile(states):
        kb, vb = kv(i - 1)
        return tuple(update(st, scores(qc, kb, tab_ref[0, 0], None), vb) for st, qc in zip(states, qs))

    states = lax.cond(i > 0, sub_tile, lambda st: st, tuple(states))

    def body(j, states):
        kb, vb = kv(j)
        return tuple(update(st, scores(qc, kb, None, None), vb) for st, qc in zip(states, qs))

    states = lax.fori_loop(0, jnp.maximum(i - 1, 0), body, states)
    (_, l1, acc1), (_, l2, acc2) = states
    lam = _df_lambda(lv_ref, lam_init)
    o = acc1 / l1 - lam * (acc2 / l2)
    o = o * lax.rsqrt(jnp.mean(o * o, axis=1, keepdims=True) + 1e-5)
    o_ref[0] = (o * g_ref[...] * (1.0 - lam_init)).astype(o_ref.dtype)


def _df_prompt_attention(q, k, v, tab, lam_vec, subln_g, batch, seq, lam_init):
    d = q.shape[1]
    t = tab.shape[2]
    q3, k3, v3 = (a.reshape(batch, seq, d) for a in (q, k, v))
    kv_spec = pl.BlockSpec((1, seq, LANES), lambda b, h, i: (b, 0, h))
    q_spec = pl.BlockSpec((1, t, LANES), lambda b, h, i: (b, i, h))
    out = pl.pallas_call(
        functools.partial(_df_prompt_kernel, lam_init=lam_init),
        grid=(batch, d // LANES, seq // t),
        in_specs=[pl.BlockSpec(lam_vec.shape, lambda b, h, i: (0, 0)),
                  pl.BlockSpec((1, LANES), lambda b, h, i: (0, 0)),
                  q_spec, kv_spec, kv_spec,
                  pl.BlockSpec((1, 2, t, t), lambda b, h, i: (h, 0, 0, 0))],
        out_specs=q_spec,
        out_shape=jax.ShapeDtypeStruct((batch, seq, d), BF16),
        compiler_params=_params("parallel", "parallel", "arbitrary"),
        name="df_prompt",
    )(lam_vec, subln_g.reshape(1, LANES), q3, k3, v3, tab)
    return out.reshape(batch * seq, d)


def _head_rows(q_row):
    d = q_row.shape[1]
    q = jnp.broadcast_to(q_row.astype(F32), (DEC_HEADS, d))
    row = lax.broadcasted_iota(jnp.int32, (DEC_HEADS, d), 0)
    col = lax.broadcasted_iota(jnp.int32, (DEC_HEADS, d), 1)
    return jnp.where(col // HEAD_DIM == row, q, 0.0).astype(BF16)


def _sb_decode_kernel(pt_ref, q_ref, *refs):
    del pt_ref
    npg = (len(refs) - 3) // 2
    k_refs, v_refs = refs[:npg], refs[npg:2 * npg]
    o_ref, acc_ref, c_ref = refs[2 * npg:]
    s = pl.program_id(1)
    d = q_ref.shape[2]

    @pl.when(s == 0)
    def _():
        acc_ref[...] = jnp.zeros_like(acc_ref)
        c_ref[...] = jnp.zeros_like(c_ref)

    qrows = _head_rows(q_ref[0])
    u = _suffix_matrix(PAGE_SIZE)
    acc = acc_ref[...]
    cc = c_ref[...]
    for p in range(npg):
        kb = k_refs[p][0].astype(BF16)
        vb = v_refs[p][0].astype(BF16)
        z = _dot_nt(qrows, kb)
        nls = _softplus2(z)
        after = _dot(nls.astype(BF16), u)
        a = jnp.exp2(z - nls + after + cc)
        acc = acc + _dot(a.astype(BF16), vb)
        cc = cc + after[:, 0:1] - _bf16_round(nls[:, 0:1])
    acc_ref[...] = acc
    c_ref[...] = cc

    @pl.when(s == pl.num_programs(1) - 1)
    def _():
        row = lax.broadcasted_iota(jnp.int32, (DEC_HEADS, d), 0)
        col = lax.broadcasted_iota(jnp.int32, (DEC_HEADS, d), 1)
        own = col // HEAD_DIM == row
        o_ref[0] = jnp.sum(jnp.where(own, acc, 0.0), axis=0, keepdims=True).astype(o_ref.dtype)


def _page_specs(n_pages, npg, d, reverse, first_page):
    specs = []
    for p in range(npg):
        if reverse:
            idx = lambda b, s, pt, p=p: (first_page + pt[b, n_pages - 1 - (s * npg + p)], 0, 0)
        else:
            idx = lambda b, s, pt, p=p: (first_page + pt[b, s * npg + p], 0, 0)
        specs.append(pl.BlockSpec((1, PAGE_SIZE, d), idx))
    return specs


def _sb_decode_attention(q, cache_k, cache_v, page_table, first_page):
    db, d = q.shape
    n_pages = page_table.shape[1]
    npg = min(DEC_PAGES_PER_STEP, n_pages)
    row = pl.BlockSpec((1, 1, d), lambda b, s, pt: (b, 0, 0))
    pages = lambda: _page_specs(n_pages, npg, d, True, first_page)
    out = pl.pallas_call(
        _sb_decode_kernel,
        grid_spec=pltpu.PrefetchScalarGridSpec(
            num_scalar_prefetch=1,
            grid=(db, n_pages // npg),
            in_specs=[row] + pages() + pages(),
            out_specs=row,
            scratch_shapes=[pltpu.VMEM((DEC_HEADS, d), F32), pltpu.VMEM((DEC_HEADS, 1), F32)]),
        out_shape=jax.ShapeDtypeStruct((db, 1, d), BF16),
        compiler_params=_params("parallel", "arbitrary"),
        name="sb_decode",
    )(page_table, q.reshape(db, 1, d), *([cache_k] * npg), *([cache_v] * npg))
    return out.reshape(db, d)


def _df_decode_kernel(pt_ref, lv_ref, g_ref, dec_ref, new_ref, q_ref, kn_ref, vn_ref, *refs, lam_init):
    del pt_ref
    npg = (len(refs) - 4) // 2
    k_refs, v_refs = refs[:npg], refs[npg:2 * npg]
    o_ref, acc_ref, m_ref, l_ref = refs[2 * npg:]
    s = pl.program_id(1)
    last = s == pl.num_programs(1) - 1
    d = q_ref.shape[2]

    @pl.when(s == 0)
    def _():
        acc_ref[...] = jnp.zeros_like(acc_ref)
        m_ref[...] = jnp.full_like(m_ref, NEG_BIG)
        l_ref[...] = jnp.zeros_like(l_ref)

    qrows = _head_rows(q_ref[0])
    scores = []
    vbs = []
    for p in range(npg):
        sp = _dot_nt(qrows, k_refs[p][0].astype(BF16))
        if p == npg - 1:
            sp = sp + jnp.where(last, dec_ref[...], 0.0)
        scores.append(sp)
        vbs.append(v_refs[p][0].astype(BF16))
    m_old = m_ref[...]
    m_new = m_old
    for sp in scores:
        m_new = jnp.maximum(m_new, jnp.max(sp, axis=1, keepdims=True))
    alpha = jnp.exp2(m_old - m_new)
    l = alpha * l_ref[...]
    acc = alpha * acc_ref[...]
    for sp, vb in zip(scores, vbs):
        pr = jnp.exp2(sp - m_new)
        l = l + jnp.sum(pr, axis=1, keepdims=True)
        acc = acc + _dot(pr.astype(BF16), vb)
    m_ref[...] = m_new
    l_ref[...] = l
    acc_ref[...] = acc

    @pl.when(last)
    def _():
        row = lax.broadcasted_iota(jnp.int32, (DEC_HEADS, d), 0)
        col = lax.broadcasted_iota(jnp.int32, (DEC_HEADS, d), 1)
        own = col // LANES == row // 2
        kn = kn_ref[0].astype(F32)
        s_new = jnp.sum(qrows.astype(F32) * kn, axis=1, keepdims=True) + new_ref[...][:, 0:1]
        m_fin = jnp.maximum(m_new, s_new)
        a_old = jnp.exp2(m_new - m_fin)
        p_new = jnp.exp2(s_new - m_fin)
        l_fin = a_old * l + p_new
        acc_fin = a_old * acc + p_new * vn_ref[0].astype(F32)
        lam = _df_lambda(lv_ref, lam_init)
        coef = jnp.where(row[:, 0:1] % 2 == 0, 1.0, -lam) / l_fin
        o_all = jnp.sum(jnp.where(own, acc_fin * coef, 0.0), axis=0, keepdims=True)
        o_rows = jnp.where(own, jnp.broadcast_to(o_all, (DEC_HEADS, d)), 0.0)
        ms = jnp.sum(o_rows * o_rows, axis=1, keepdims=True) * (1.0 / LANES)
        y_rows = jnp.where(row % 2 == 0, o_rows * lax.rsqrt(ms + 1e-5), 0.0)
        y = jnp.sum(y_rows, axis=0, keepdims=True)
        o_ref[0] = (y * g_ref[...] * (1.0 - lam_init)).astype(o_ref.dtype)


def _df_decode_attention(q, k_new, v_new, cache_k, cache_v, page_table, first_page, dec_bias, new_bias,
                         lam_vec, subln_g, lam_init):
    db, d = q.shape
    n_pages = page_table.shape[1]
    npg = min(DEC_PAGES_PER_STEP, n_pages)
    n_heads = d // LANES
    row = pl.BlockSpec((1, 1, d), lambda b, s, pt: (b, 0, 0))
    const = lambda shape: pl.BlockSpec(shape, lambda b, s, pt: (0,) * len(shape))
    pages = lambda: _page_specs(n_pages, npg, d, False, first_page)
    g_row = jnp.tile(subln_g.reshape(1, LANES), (1, n_heads))
    out = pl.pallas_call(
        functools.partial(_df_decode_kernel, lam_init=lam_init),
        grid_spec=pltpu.PrefetchScalarGridSpec(
            num_scalar_prefetch=1,
            grid=(db, n_pages // npg),
            in_specs=[const(lam_vec.shape), const((1, d)), const((DEC_HEADS, PAGE_SIZE)),
                      const((DEC_HEADS, LANES)), row, row, row] + pages() + pages(),
            out_specs=row,
            scratch_shapes=[pltpu.VMEM((DEC_HEADS, d), F32), pltpu.VMEM((DEC_HEADS, 1), F32),
                            pltpu.VMEM((DEC_HEADS, 1), F32)]),
        out_shape=jax.ShapeDtypeStruct((db, 1, d), BF16),
        compiler_params=_params("parallel", "arbitrary"),
        name="df_decode",
    )(page_table, lam_vec, g_row, dec_bias, new_bias, q.reshape(db, 1, d),
      k_new.reshape(db, 1, d), v_new.reshape(db, 1, d), *([cache_k] * npg), *([cache_v] * npg))
    return out.reshape(db, d)


def kernel(x_prompt, x_sample, cache_sb_k, cache_sb_v, cache_df_k, cache_df_v, page_table,
           w_in, w_out, ln_g, ln_b, df_lambda, df_subln_g, rel_bias, w_up, w_down):
    batch, seq, d = x_prompt.shape
    db = x_sample.shape[0]
    depth = w_in.shape[0]
    n_sb_heads = d // HEAD_DIM
    n_df_heads = d // (2 * HEAD_DIM)
    alpha = (2 * depth) ** 0.25
    q_scale = ATT_SCALE * LOG2E
    assert x_sample.shape[1] == 1 and d // HEAD_DIM == DEC_HEADS
    assert cache_sb_k.shape[2] == PAGE_SIZE and seq % min(ATT_BLOCK, seq) == 0

    xp = x_prompt.reshape(batch * seq, d)
    xs = x_sample.reshape(db, d)
    w_in_b, w_out_b = w_in.astype(BF16), w_out.astype(BF16)
    w_up_b, w_down_b = w_up.astype(BF16), w_down.astype(BF16)
    n_pool = cache_sb_k.shape[1]

    sb_kp, sb_vp, sb_ks, sb_vs = [], [], [], []
    df_kp, df_vp, df_ks, df_vs = [], [], [], []
    for i in range(depth):
        j = i // 2
        qp, kp, vp, kpb, vpb = _qkv_proj(xp, w_in_b[i], q_scale)
        qs, ks_, vs_, _, _ = _qkv_proj(xs, w_in_b[i], q_scale)
        if i % 2 == 0:
            op = _sb_prompt_attention(qp, kpb, vpb, batch, seq)
            os_ = _sb_decode_attention(qs, cache_sb_k.reshape(-1, PAGE_SIZE, d),
                                       cache_sb_v.reshape(-1, PAGE_SIZE, d), page_table, j * n_pool)
            sb_kp.append(kp.reshape(batch, seq, n_sb_heads, HEAD_DIM))
            sb_vp.append(vp.reshape(batch, seq, n_sb_heads, HEAD_DIM))
            sb_ks.append(ks_.reshape(db, 1, n_sb_heads, HEAD_DIM))
            sb_vs.append(vs_.reshape(db, 1, n_sb_heads, HEAD_DIM))
        else:
            lam_init = 0.8 - 0.6 * math.exp(-0.3 * i)
            tab, dec_bias, new_bias = _bias_tables(rel_bias, min(ATT_BLOCK, seq))
            dec_bias = jnp.repeat(dec_bias.reshape(n_df_heads, PAGE_SIZE), 2, axis=0)
            new_bias = jnp.repeat(new_bias.reshape(n_df_heads, LANES), 2, axis=0)
            op = _df_prompt_attention(qp, kpb, vpb, tab, df_lambda[j], df_subln_g[j], batch, seq, lam_init)
            os_ = _df_decode_attention(qs, ks_, vs_, cache_df_k.reshape(-1, PAGE_SIZE, d),
                                       cache_df_v.reshape(-1, PAGE_SIZE, d), page_table, j * n_pool,
                                       dec_bias, new_bias, df_lambda[j], df_subln_g[j], lam_init)
            df_kp.append(kp.reshape(batch, seq, n_df_heads, 2, HEAD_DIM))
            df_vp.append(vp.reshape(batch, seq, n_df_heads, 2 * HEAD_DIM))
            df_ks.append(ks_.reshape(db, 1, n_df_heads, 2, HEAD_DIM))
            df_vs.append(vs_.reshape(db, 1, n_df_heads, 2 * HEAD_DIM))
        xp = _outproj_ln(op, xp, w_out_b[i], ln_g[i, 0], ln_b[i, 0], alpha)
        xs = _outproj_ln(os_, xs, w_out_b[i], ln_g[i, 0], ln_b[i, 0], alpha)
        xp = _mlp_ln(xp, w_up_b[i], w_down_b[i], ln_g[i, 1], ln_b[i, 1], alpha)
        xs = _mlp_ln(xs, w_up_b[i], w_down_b[i], ln_g[i, 1], ln_b[i, 1], alpha)
    return (xp.reshape(batch, seq, d), xs.reshape(db, 1, d),
            jnp.stack(sb_kp), jnp.stack(sb_vp), jnp.stack(sb_ks), jnp.stack(sb_vs),
            jnp.stack(df_kp), jnp.stack(df_vp), jnp.stack(df_ks), jnp.stack(df_vs))
```

```python
import functools
import math

import numpy as np
import jax
import jax.numpy as jnp
from jax import lax
from jax.experimental import pallas as pl
from jax.experimental.pallas import tpu as pltpu

HEAD_DIM = 64
PAGE_SIZE = 128
REL_BUCKETS = 32
REL_MAX_DIST = 128
ATT_SCALE = HEAD_DIM ** -0.5
NEG_BIG = -1e30
LN_EPS = 1e-5
LOG2E = math.log2(math.e)

LANES = 128
MXU_DIM = 256
VMEM_LIMIT_BYTES = 48 * 1024 * 1024

ATT_BLOCK = 2 * MXU_DIM
ROW_BLOCK = 512
FF_CHUNK = 1024
DEC_HEADS = 16
DEC_PAGES_PER_STEP = 8

F32 = jnp.float32
BF16 = jnp.bfloat16


def _params(*semantics):
    return pltpu.CompilerParams(dimension_semantics=semantics,
                                vmem_limit_bytes=VMEM_LIMIT_BYTES)


def _dot(a, b):
    return jnp.dot(a, b, preferred_element_type=F32)


def _dot_nt(a, b):
    return lax.dot_general(a, b, (((1,), (1,)), ((), ())), preferred_element_type=F32)


def _suffix_matrix_np(n):
    return np.where(np.arange(n)[:, None] > np.arange(n)[None, :], -1.0, 0.0)


def _suffix_matrix(n):
    r = lax.broadcasted_iota(jnp.int32, (n, n), 0)
    c = lax.broadcasted_iota(jnp.int32, (n, n), 1)
    return jnp.where(r > c, -1.0, 0.0).astype(BF16)


def _split_halves(q2):
    qf = q2.astype(F32)
    first = lax.broadcasted_iota(jnp.int32, qf.shape, 1) < HEAD_DIM
    return jnp.where(first, qf, 0.0).astype(BF16), jnp.where(first, 0.0, qf).astype(BF16)


def _bf16_round(x):
    return x.astype(BF16).astype(F32)


def _rep(col):
    return jnp.broadcast_to(col, (col.shape[0], LANES))


def _lane_blocks(x):
    return [x[:, cb * LANES:(cb + 1) * LANES] for cb in range(x.shape[1] // LANES)]


def _softplus2(z):
    neg_abs = pltpu.bitcast(pltpu.bitcast(z, jnp.uint32) | jnp.uint32(0x80000000), F32)
    return jnp.maximum(z, 0.0) + jnp.log2(1.0 + jnp.exp2(neg_abs))


def _proj_kernel(x_ref, w_ref, q_ref, k_ref, v_ref, kb_ref, vb_ref, *, q_scale):
    d = x_ref.shape[1]
    x = x_ref[...].astype(BF16)
    q = _dot(x, w_ref[:, 0:d])
    q_ref[...] = (q * q_scale).astype(BF16)
    k = _dot(x, w_ref[:, d:2 * d])
    k_ref[...] = k
    kb_ref[...] = k.astype(BF16)
    v = _dot(x, w_ref[:, 2 * d:3 * d])
    v_ref[...] = v
    vb_ref[...] = v.astype(BF16)


def _qkv_proj(x, w_bf16, q_scale):
    m, d = x.shape
    bm = min(ROW_BLOCK, m)
    row = pl.BlockSpec((bm, d), lambda i: (i, 0))
    return pl.pallas_call(
        functools.partial(_proj_kernel, q_scale=q_scale),
        grid=(m // bm,),
        in_specs=[row, pl.BlockSpec((d, 3 * d), lambda i: (0, 0))],
        out_specs=[row, row, row, row, row],
        out_shape=[jax.ShapeDtypeStruct((m, d), BF16),
                   jax.ShapeDtypeStruct((m, d), F32),
                   jax.ShapeDtypeStruct((m, d), F32),
                   jax.ShapeDtypeStruct((m, d), BF16),
                   jax.ShapeDtypeStruct((m, d), BF16)],
        compiler_params=_params("parallel"),
        name="qkv_proj",
    )(x, w_bf16)


def _layernorm(xf, g, b):
    mu = jnp.mean(xf, axis=-1, keepdims=True)
    xc = xf - mu
    var = jnp.mean(xc * xc, axis=-1, keepdims=True)
    return xc * lax.rsqrt(var + LN_EPS) * g + b


def _outproj_ln_kernel(o_ref, x_ref, w_ref, g_ref, b_ref, y_ref, *, alpha):
    m = _dot(o_ref[...], w_ref[...])
    y_ref[...] = _layernorm(alpha * x_ref[...] + m, g_ref[...], b_ref[...])


def _outproj_ln(o_bf16, x, w_bf16, g, b, alpha):
    m, d = x.shape
    bm = min(ROW_BLOCK, m)
    row = pl.BlockSpec((bm, d), lambda i: (i, 0))
    vec = pl.BlockSpec((1, d), lambda i: (0, 0))
    return pl.pallas_call(
        functools.partial(_outproj_ln_kernel, alpha=alpha),
        grid=(m // bm,),
        in_specs=[row, row, pl.BlockSpec((d, d), lambda i: (0, 0)), vec, vec],
        out_specs=row,
        out_shape=jax.ShapeDtypeStruct((m, d), F32),
        compiler_params=_params("parallel"),
        name="outproj_ln",
    )(o_bf16, x, w_bf16, g.reshape(1, d), b.reshape(1, d))


def _mlp_ln_kernel(x_ref, wu_ref, wd_ref, g_ref, b_ref, y_ref, acc_ref, *, alpha):
    c = pl.program_id(1)
    x = x_ref[...]
    h = jnp.maximum(_dot(x.astype(BF16), wu_ref[...]), 0.0)
    part = _dot((h * h).astype(BF16), wd_ref[...])

    @pl.when(c == 0)
    def _():
        acc_ref[...] = part

    @pl.when(c > 0)
    def _():
        acc_ref[...] += part

    @pl.when(c == pl.num_programs(1) - 1)
    def _():
        y_ref[...] = _layernorm(alpha * x + acc_ref[...], g_ref[...], b_ref[...])


def _mlp_ln(x, wu_bf16, wd_bf16, g, b, alpha):
    m, d = x.shape
    dff = wu_bf16.shape[1]
    bm = min(ROW_BLOCK, m)
    fc = min(FF_CHUNK, dff)
    row = pl.BlockSpec((bm, d), lambda i, c: (i, 0))
    vec = pl.BlockSpec((1, d), lambda i, c: (0, 0))
    return pl.pallas_call(
        functools.partial(_mlp_ln_kernel, alpha=alpha),
        grid=(m // bm, dff // fc),
        in_specs=[row,
                  pl.BlockSpec((d, fc), lambda i, c: (0, c)),
                  pl.BlockSpec((fc, d), lambda i, c: (c, 0)),
                  vec, vec],
        out_specs=row,
        out_shape=jax.ShapeDtypeStruct((m, d), F32),
        scratch_shapes=[pltpu.VMEM((bm, d), F32)],
        compiler_params=_params("parallel", "arbitrary"),
        name="mlp_ln",
    )(x, wu_bf16, wd_bf16, g.reshape(1, d), b.reshape(1, d))


def _sb_scores(qh, kb, mask, nls_ref, nls0_ref, logb_ref):
    z = _dot_nt(qh, kb)
    nls = _softplus2(z)
    logb = z - nls
    if mask is not None:
        nls = jnp.where(mask, nls, 0.0)
        logb = jnp.where(mask, logb, NEG_BIG)
    nls_ref[...] = nls.astype(BF16)
    for ch in range(nls.shape[1] // MXU_DIM):
        nls0_ref[ch] = _rep(_bf16_round(nls[:, ch * MXU_DIM:ch * MXU_DIM + 1]))
    logb_ref[...] = logb


def _sb_weights(nls_ref, nls0_ref, logb_ref, vb, u, acc_ref, c_ref):
    t = nls_ref.shape[0]
    n_chunks = nls_ref.shape[1] // MXU_DIM
    later = jnp.zeros((t, LANES), F32)
    a = [None] * n_chunks
    for ch in reversed(range(n_chunks)):
        cols = slice(ch * MXU_DIM, (ch + 1) * MXU_DIM)
        after = _dot(nls_ref[:, cols], u)
        blocks = []
        for lb in range(MXU_DIM // LANES):
            lanes = slice(lb * LANES, (lb + 1) * LANES)
            e = logb_ref[:, cols][:, lanes] + after[:, lanes]
            blocks.append(jnp.exp2(e if ch == n_chunks - 1 else e + later).astype(BF16))
        a[ch] = jnp.concatenate(blocks, axis=1)
        later = later + (_rep(after[:, 0:1]) - nls0_ref[ch])
    o = _dot(jnp.concatenate(a, axis=1), vb)
    c = c_ref[...]
    acc_ref[...] += jnp.exp2(c) * o
    c_ref[...] = c + later


def _sb_prompt_kernel(u_ref, q_ref, k_ref, v_ref, o_ref, qs_ref, nls_ref, nls0_ref, logb_ref, acc_ref, c_ref):
    t = q_ref.shape[1]
    i = pl.program_id(2)
    qs_ref[0], qs_ref[1] = _split_halves(q_ref[0])
    acc_ref[...] = jnp.zeros_like(acc_ref)
    c_ref[...] = jnp.zeros_like(c_ref)

    def tile_rows(ref, j):
        return ref[0, pl.ds(pl.multiple_of(j * t, t), t), :]

    def scores(j, slot, mask=None):
        kb = tile_rows(k_ref, j)
        for h in range(2):
            _sb_scores(qs_ref[h], kb, mask, nls_ref.at[slot, h], nls0_ref.at[slot, h], logb_ref.at[slot, h])

    def weights(j, slot):
        vb = tile_rows(v_ref, j)
        for h in range(2):
            _sb_weights(nls_ref.at[slot, h], nls0_ref.at[slot, h], logb_ref.at[slot, h], vb, u_ref[...],
                        acc_ref.at[h], c_ref.at[h])

    r = lax.broadcasted_iota(jnp.int32, (t, t), 0)
    c = lax.broadcasted_iota(jnp.int32, (t, t), 1)
    scores(i, 0, c < r)

    def body(s, _):
        j = i - 1 - 2 * s
        scores(j, 1)
        weights(j + 1, 0)
        scores(j - 1, 0)
        weights(j, 1)
        return 0

    lax.fori_loop(0, i // 2, body, 0)

    @pl.when(i % 2 == 1)
    def _():
        scores(0, 1)
        weights(1, 0)
        weights(0, 1)

    @pl.when(i % 2 == 0)
    def _():
        weights(0, 0)

    lane_o = lax.broadcasted_iota(jnp.int32, (t, LANES), 1)
    o_ref[0] = jnp.where(lane_o < HEAD_DIM, acc_ref[0], acc_ref[1]).astype(o_ref.dtype)


def _sb_prompt_attention(q, k, v, batch, seq):
    d = q.shape[1]
    t = min(ATT_BLOCK, seq)
    q3, k3, v3 = (a.reshape(batch, seq, d) for a in (q, k, v))
    kv_spec = pl.BlockSpec((1, seq, LANES), lambda b, h, i: (b, 0, h))
    q_spec = pl.BlockSpec((1, t, LANES), lambda b, h, i: (b, i, h))
    out = pl.pallas_call(
        _sb_prompt_kernel,
        grid=(batch, d // LANES, seq // t),
        in_specs=[pl.BlockSpec((MXU_DIM, MXU_DIM), lambda b, h, i: (0, 0)), q_spec, kv_spec, kv_spec],
        out_specs=q_spec,
        out_shape=jax.ShapeDtypeStruct((batch, seq, d), BF16),
        scratch_shapes=[pltpu.VMEM((2, t, LANES), BF16),
                        pltpu.VMEM((2, 2, t, t), BF16),
                        pltpu.VMEM((2, 2, t // MXU_DIM, t, LANES), F32),
                        pltpu.VMEM((2, 2, t, t), F32),
                        pltpu.VMEM((2, t, LANES), F32),
                        pltpu.VMEM((2, t, LANES), F32)],
        compiler_params=_params("parallel", "parallel", "arbitrary"),
        name="sb_prompt",
    )(jnp.asarray(_suffix_matrix_np(MXU_DIM), BF16), q3, k3, v3)
    return out.reshape(batch * seq, d)


def _t5_bucket_np(dist):
    n = np.maximum(dist, 0).astype(np.int32)
    max_exact = REL_BUCKETS // 2
    nf = np.maximum(n, 1).astype(np.float32)
    large = max_exact + (np.log(nf / np.float32(max_exact)) / np.float32(math.log(REL_MAX_DIST / max_exact))
                         * np.float32(REL_BUCKETS - max_exact)).astype(np.int32)
    large = np.minimum(large, REL_BUCKETS - 1)
    return np.where(n < max_exact, n, large).astype(np.int32)


def _bias_table_kernel(rel_ref, bsub_ref, bdiag_ref, bdec_ref, tab_ref, dec_ref, new_ref):
    h = pl.program_id(0)
    far = rel_ref[REL_BUCKETS - 1, h]

    def gather(bucket):
        out = jnp.zeros(bucket.shape, F32)
        for b in range(REL_BUCKETS):
            out = jnp.where(bucket == b, rel_ref[b, h], out)
        return (out - far) * LOG2E

    tab_ref[0, 0] = gather(bsub_ref[...])
    tab_ref[0, 1] = gather(bdiag_ref[...])
    dec_ref[0] = gather(bdec_ref[...])
    new_ref[0] = jnp.zeros(new_ref.shape[1:], F32) + (rel_ref[0, h] - far) * LOG2E


def _bias_tables(rel_bias, t):
    assert t >= REL_MAX_DIST and PAGE_SIZE >= REL_MAX_DIST
    n_heads = rel_bias.shape[1]
    r = np.arange(t)[:, None]
    c = np.arange(t)[None, :]
    b_sub = _t5_bucket_np(r - c + t)
    b_diag = _t5_bucket_np(r - c)
    b_dec = _t5_bucket_np(PAGE_SIZE - np.arange(PAGE_SIZE))[None, :]
    full = lambda shape: pl.BlockSpec(shape, lambda h: (0,) * len(shape))
    return pl.pallas_call(
        _bias_table_kernel,
        grid=(n_heads,),
        in_specs=[pl.BlockSpec(memory_space=pltpu.SMEM), full((t, t)), full((t, t)), full((1, PAGE_SIZE))],
        out_specs=[pl.BlockSpec((1, 2, t, t), lambda h: (h, 0, 0, 0)),
                   pl.BlockSpec((1, 1, PAGE_SIZE), lambda h: (h, 0, 0)),
                   pl.BlockSpec((1, 1, LANES), lambda h: (h, 0, 0))],
        out_shape=[jax.ShapeDtypeStruct((n_heads, 2, t, t), F32),
                   jax.ShapeDtypeStruct((n_heads, 1, PAGE_SIZE), F32),
                   jax.ShapeDtypeStruct((n_heads, 1, LANES), F32)],
        compiler_params=_params("arbitrary"),
        name="t5_bias_tables",
    )(rel_bias, jnp.asarray(b_sub), jnp.asarray(b_diag), jnp.asarray(b_dec))


def _df_lambda(lv_ref, lam_init):
    lv = lv_ref[...]
    a = jnp.sum(lv[0:1, :] * lv[1:2, :], axis=1, keepdims=True)
    b = jnp.sum(lv[2:3, :] * lv[3:4, :], axis=1, keepdims=True)
    return jnp.exp(a) - jnp.exp(b) + lam_init


def _df_prompt_kernel(lv_ref, g_ref, q_ref, k_ref, v_ref, tab_ref, o_ref,
                      qs_ref, s_ref, rmax_ref, m_ref, l_ref, acc_ref, *, lam_init):
    t = q_ref.shape[1]
    i = pl.program_id(2)
    qs_ref[0], qs_ref[1] = _split_halves(q_ref[0])
    m_ref[...] = jnp.full_like(m_ref, NEG_BIG)
    l_ref[...] = jnp.zeros_like(l_ref)
    acc_ref[...] = jnp.zeros_like(acc_ref)

    def tile_rows(ref, j):
        return ref[0, pl.ds(pl.multiple_of(j * t, t), t), :]

    def scores(j, slot, bias=None, mask=None):
        kb = tile_rows(k_ref, j)
        for c in range(2):
            s = _dot_nt(qs_ref[c], kb)
            if bias is not None:
                s = s + bias
            if mask is not None:
                s = jnp.where(mask, s, NEG_BIG)
            s_ref[slot, c] = s
            rmax_ref[slot, c] = _rep(jnp.max(s, axis=1, keepdims=True))

    def update(j, slot):
        vb = tile_rows(v_ref, j)
        for c in range(2):
            m = m_ref[c]
            m_new = jnp.maximum(m, rmax_ref[slot, c])
            alpha = jnp.exp2(m - m_new)
            p = [jnp.exp2(sb - m_new) for sb in _lane_blocks(s_ref[slot, c])]
            psum = p[0]
            for pb in p[1:]:
                psum = psum + pb
            l_ref[c] = alpha * l_ref[c] + _rep(jnp.sum(psum, axis=1, keepdims=True))
            pv = _dot(jnp.concatenate([pb.astype(BF16) for pb in p], axis=1), vb)
            acc_ref[c] = alpha * acc_ref[c] + pv
            m_ref[c] = m_new

    r = lax.broadcasted_iota(jnp.int32, (t, t), 0)
    c = lax.broadcasted_iota(jnp.int32, (t, t), 1)
    scores(i, 0, tab_ref[0, 1], c <= r)

    @pl.when(i > 0)
    def _():
        scores(i - 1, 1, tab_ref[0, 0])
        update(i, 0)

    def body(s, _):
        j = i - 2 - 2 * s
        scores(j, 0)
        update(j + 1, 1)
        scores(j - 1, 1)
        update(j, 0)
        return 0

    n_far = jnp.maximum(i - 1, 0)
    lax.fori_loop(0, n_far // 2, body, 0)

    @pl.when(n_far % 2 == 1)
    def _():
        scores(0, 0)
        update(1, 1)
        update(0, 0)

    @pl.when((n_far % 2 == 0) & (i > 0))
    def _():
        update(0, 1)

    @pl.when(i == 0)
    def _():
        update(0, 0)

    lam = _df_lambda(lv_ref, lam_init)
    o = acc_ref[0] / l_ref[0] - lam * (acc_ref[1] / l_ref[1])
    o = o * lax.rsqrt(jnp.mean(o * o, axis=1, keepdims=True) + 1e-5)
    o_ref[0] = (o * g_ref[...] * (1.0 - lam_init)).astype(o_ref.dtype)


def _df_prompt_attention(q, k, v, tab, lam_vec, subln_g, batch, seq, lam_init):
    d = q.shape[1]
    t = tab.shape[2]
    q3, k3, v3 = (a.reshape(batch, seq, d) for a in (q, k, v))
    kv_spec = pl.BlockSpec((1, seq, LANES), lambda b, h, i: (b, 0, h))
    q_spec = pl.BlockSpec((1, t, LANES), lambda b, h, i: (b, i, h))
    out = pl.pallas_call(
        functools.partial(_df_prompt_kernel, lam_init=lam_init),
        grid=(batch, d // LANES, seq // t),
        in_specs=[pl.BlockSpec(lam_vec.shape, lambda b, h, i: (0, 0)),
                  pl.BlockSpec((1, LANES), lambda b, h, i: (0, 0)),
                  q_spec, kv_spec, kv_spec,
                  pl.BlockSpec((1, 2, t, t), lambda b, h, i: (h, 0, 0, 0))],
        out_specs=q_spec,
        out_shape=jax.ShapeDtypeStruct((batch, seq, d), BF16),
        scratch_shapes=[pltpu.VMEM((2, t, LANES), BF16),
                        pltpu.VMEM((2, 2, t, t), F32),
                        pltpu.VMEM((2, 2, t, LANES), F32),
                        pltpu.VMEM((2, t, LANES), F32),
                        pltpu.VMEM((2, t, LANES), F32),
                        pltpu.VMEM((2, t, LANES), F32)],
        compiler_params=_params("parallel", "parallel", "arbitrary"),
        name="df_prompt",
    )(lam_vec, subln_g.reshape(1, LANES), q3, k3, v3, tab)
    return out.reshape(batch * seq, d)


def _head_rows(q_row):
    d = q_row.shape[1]
    q = jnp.broadcast_to(q_row.astype(F32), (DEC_HEADS, d))
    row = lax.broadcasted_iota(jnp.int32, (DEC_HEADS, d), 0)
    col = lax.broadcasted_iota(jnp.int32, (DEC_HEADS, d), 1)
    return jnp.where(col // HEAD_DIM == row, q, 0.0).astype(BF16)


def _sb_decode_kernel(pt_ref, q_ref, *refs):
    del pt_ref
    npg = (len(refs) - 3) // 2
    k_refs, v_refs = refs[:npg], refs[npg:2 * npg]
    o_ref, acc_ref, c_ref = refs[2 * npg:]
    s = pl.program_id(1)
    d = q_ref.shape[2]

    @pl.when(s == 0)
    def _():
        acc_ref[...] = jnp.zeros_like(acc_ref)
        c_ref[...] = jnp.zeros_like(c_ref)

    qrows = _head_rows(q_ref[0])
    u = _suffix_matrix(PAGE_SIZE)
    acc = acc_ref[...]
    cc = c_ref[...]
    for p in range(npg):
        z = _dot_nt(qrows, k_refs[p][0])
        nls = _softplus2(z)
        after = _dot(nls.astype(BF16), u)
        a = jnp.exp2(z - nls + after + cc)
        acc = acc + _dot(a.astype(BF16), v_refs[p][0])
        cc = cc + after[:, 0:1] - _bf16_round(nls[:, 0:1])
    acc_ref[...] = acc
    c_ref[...] = cc

    @pl.when(s == pl.num_programs(1) - 1)
    def _():
        row = lax.broadcasted_iota(jnp.int32, (DEC_HEADS, d), 0)
        col = lax.broadcasted_iota(jnp.int32, (DEC_HEADS, d), 1)
        own = col // HEAD_DIM == row
        o_ref[0] = jnp.sum(jnp.where(own, acc, 0.0), axis=0, keepdims=True).astype(o_ref.dtype)


def _page_specs(n_pages, npg, d, reverse, first_page):
    specs = []
    for p in range(npg):
        if reverse:
            idx = lambda b, s, pt, p=p: (first_page + pt[b, n_pages - 1 - (s * npg + p)], 0, 0)
        else:
            idx = lambda b, s, pt, p=p: (first_page + pt[b, s * npg + p], 0, 0)
        specs.append(pl.BlockSpec((1, PAGE_SIZE, d), idx))
    return specs


def _sb_decode_attention(q, cache_k, cache_v, page_table, first_page):
    db, d = q.shape
    n_pages = page_table.shape[1]
    npg = min(DEC_PAGES_PER_STEP, n_pages)
    row = pl.BlockSpec((1, 1, d), lambda b, s, pt: (b, 0, 0))
    pages = lambda: _page_specs(n_pages, npg, d, True, first_page)
    out = pl.pallas_call(
        _sb_decode_kernel,
        grid_spec=pltpu.PrefetchScalarGridSpec(
            num_scalar_prefetch=1,
            grid=(db, n_pages // npg),
            in_specs=[row] + pages() + pages(),
            out_specs=row,
            scratch_shapes=[pltpu.VMEM((DEC_HEADS, d), F32), pltpu.VMEM((DEC_HEADS, 1), F32)]),
        out_shape=jax.ShapeDtypeStruct((db, 1, d), BF16),
        compiler_params=_params("parallel", "arbitrary"),
        name="sb_decode",
    )(page_table, q.reshape(db, 1, d), *([cache_k] * npg), *([cache_v] * npg))
    return out.reshape(db, d)


def _df_decode_kernel(pt_ref, lv_ref, g_ref, dec_ref, new_ref, q_ref, kn_ref, vn_ref, *refs, lam_init):
    del pt_ref
    npg = (len(refs) - 4) // 2
    k_refs, v_refs = refs[:npg], refs[npg:2 * npg]
    o_ref, acc_ref, m_ref, l_ref = refs[2 * npg:]
    s = pl.program_id(1)
    last = s == pl.num_programs(1) - 1
    d = q_ref.shape[2]

    @pl.when(s == 0)
    def _():
        acc_ref[...] = jnp.zeros_like(acc_ref)
        m_ref[...] = jnp.full_like(m_ref, NEG_BIG)
        l_ref[...] = jnp.zeros_like(l_ref)

    qrows = _head_rows(q_ref[0])
    scores = []
    for p in range(npg):
        sp = _dot_nt(qrows, k_refs[p][0])
        if p == npg - 1:
            sp = sp + jnp.where(last, dec_ref[...], 0.0)
        scores.append(sp)
    m_old = m_ref[...]
    m_new = m_old
    for sp in scores:
        m_new = jnp.maximum(m_new, jnp.max(sp, axis=1, keepdims=True))
    alpha = jnp.exp2(m_old - m_new)
    l = alpha * l_ref[...]
    acc = alpha * acc_ref[...]
    for p, sp in enumerate(scores):
        pr = jnp.exp2(sp - m_new)
        l = l + jnp.sum(pr, axis=1, keepdims=True)
        acc = acc + _dot(pr.astype(BF16), v_refs[p][0])
    m_ref[...] = m_new
    l_ref[...] = l
    acc_ref[...] = acc

    @pl.when(last)
    def _():
        row = lax.broadcasted_iota(jnp.int32, (DEC_HEADS, d), 0)
        col = lax.broadcasted_iota(jnp.int32, (DEC_HEADS, d), 1)
        own = col // LANES == row // 2
        kn = kn_ref[0].astype(F32)
        s_new = jnp.sum(qrows.astype(F32) * kn, axis=1, keepdims=True) + new_ref[...][:, 0:1]
        m_fin = jnp.maximum(m_new, s_new)
        a_old = jnp.exp2(m_new - m_fin)
        p_new = jnp.exp2(s_new - m_fin)
        l_fin = a_old * l + p_new
        acc_fin = a_old * acc + p_new * vn_ref[0].astype(F32)
        lam = _df_lambda(lv_ref, lam_init)
        coef = jnp.where(row[:, 0:1] % 2 == 0, 1.0, -lam) / l_fin
        o_all = jnp.sum(jnp.where(own, acc_fin * coef, 0.0), axis=0, keepdims=True)
        o_rows = jnp.where(own, jnp.broadcast_to(o_all, (DEC_HEADS, d)), 0.0)
        ms = jnp.sum(o_rows * o_rows, axis=1, keepdims=True) * (1.0 / LANES)
        y_rows = jnp.where(row % 2 == 0, o_rows * lax.rsqrt(ms + 1e-5), 0.0)
        y = jnp.sum(y_rows, axis=0, keepdims=True)
        o_ref[0] = (y * g_ref[...] * (1.0 - lam_init)).astype(o_ref.dtype)


def _df_decode_attention(q, k_new, v_new, cache_k, cache_v, page_table, first_page, dec_bias, new_bias,
                         lam_vec, subln_g, lam_init):
    db, d = q.shape
    n_pages = page_table.shape[1]
    npg = min(DEC_PAGES_PER_STEP, n_pages)
    n_heads = d // LANES
    row = pl.BlockSpec((1, 1, d), lambda b, s, pt: (b, 0, 0))
    const = lambda shape: pl.BlockSpec(shape, lambda b, s, pt: (0,) * len(shape))
    pages = lambda: _page_specs(n_pages, npg, d, False, first_page)
    g_row = jnp.tile(subln_g.reshape(1, LANES), (1, n_heads))
    out = pl.pallas_call(
        functools.partial(_df_decode_kernel, lam_init=lam_init),
        grid_spec=pltpu.PrefetchScalarGridSpec(
            num_scalar_prefetch=1,
            grid=(db, n_pages // npg),
            in_specs=[const(lam_vec.shape), const((1, d)), const((DEC_HEADS, PAGE_SIZE)),
                      const((DEC_HEADS, LANES)), row, row, row] + pages() + pages(),
            out_specs=row,
            scratch_shapes=[pltpu.VMEM((DEC_HEADS, d), F32), pltpu.VMEM((DEC_HEADS, 1), F32),
                            pltpu.VMEM((DEC_HEADS, 1), F32)]),
        out_shape=jax.ShapeDtypeStruct((db, 1, d), BF16),
        compiler_params=_params("parallel", "arbitrary"),
        name="df_decode",
    )(page_table, lam_vec, g_row, dec_bias, new_bias, q.reshape(db, 1, d),
      k_new.reshape(db, 1, d), v_new.reshape(db, 1, d), *([cache_k] * npg), *([cache_v] * npg))
    return out.reshape(db, d)


def kernel(x_prompt, x_sample, cache_sb_k, cache_sb_v, cache_df_k, cache_df_v, page_table,
           w_in, w_out, ln_g, ln_b, df_lambda, df_subln_g, rel_bias, w_up, w_down):
    batch, seq, d = x_prompt.shape
    db = x_sample.shape[0]
    depth = w_in.shape[0]
    n_sb_heads = d // HEAD_DIM
    n_df_heads = d // (2 * HEAD_DIM)
    alpha = (2 * depth) ** 0.25
    q_scale = ATT_SCALE * LOG2E
    t = min(ATT_BLOCK, seq)
    assert x_sample.shape[1] == 1 and d // HEAD_DIM == DEC_HEADS
    assert cache_sb_k.shape[2] == PAGE_SIZE and seq % t == 0 and t % MXU_DIM == 0

    xp = x_prompt.reshape(batch * seq, d)
    xs = x_sample.reshape(db, d)
    w_in_b, w_out_b = w_in.astype(BF16), w_out.astype(BF16)
    w_up_b, w_down_b = w_up.astype(BF16), w_down.astype(BF16)
    n_pool = cache_sb_k.shape[1]
    pages = lambda cache: cache.astype(BF16).reshape(-1, PAGE_SIZE, d)

    sb_kp, sb_vp, sb_ks, sb_vs = [], [], [], []
    df_kp, df_vp, df_ks, df_vs = [], [], [], []
    for i in range(depth):
        j = i // 2
        qp, kp, vp, kpb, vpb = _qkv_proj(xp, w_in_b[i], q_scale)
        qs, ks_, vs_, _, _ = _qkv_proj(xs, w_in_b[i], q_scale)
        if i % 2 == 0:
            op = _sb_prompt_attention(qp, kpb, vpb, batch, seq)
            os_ = _sb_decode_attention(qs, pages(cache_sb_k), pages(cache_sb_v), page_table, j * n_pool)
            sb_kp.append(kp.reshape(batch, seq, n_sb_heads, HEAD_DIM))
            sb_vp.append(vp.reshape(batch, seq, n_sb_heads, HEAD_DIM))
            sb_ks.append(ks_.reshape(db, 1, n_sb_heads, HEAD_DIM))
            sb_vs.append(vs_.reshape(db, 1, n_sb_heads, HEAD_DIM))
        else:
            lam_init = 0.8 - 0.6 * math.exp(-0.3 * i)
            tab, dec_bias, new_bias = _bias_tables(rel_bias, t)
            dec_bias = jnp.repeat(dec_bias.reshape(n_df_heads, PAGE_SIZE), 2, axis=0)
            new_bias = jnp.repeat(new_bias.reshape(n_df_heads, LANES), 2, axis=0)
            op = _df_prompt_attention(qp, kpb, vpb, tab, df_lambda[j], df_subln_g[j], batch, seq, lam_init)
            os_ = _df_decode_attention(qs, ks_, vs_, pages(cache_df_k), pages(cache_df_v), page_table, j * n_pool,
                                       dec_bias, new_bias, df_lambda[j], df_subln_g[j], lam_init)
            df_kp.append(kp.reshape(batch, seq, n_df_heads, 2, HEAD_DIM))
            df_vp.append(vp.reshape(batch, seq, n_df_heads, 2 * HEAD_DIM))
            df_ks.append(ks_.reshape(db, 1, n_df_heads, 2, HEAD_DIM))
            df_vs.append(vs_.reshape(db, 1, n_df_heads, 2 * HEAD_DIM))
        xp = _outproj_ln(op, xp, w_out_b[i], ln_g[i, 0], ln_b[i, 0], alpha)
        xs = _outproj_ln(os_, xs, w_out_b[i], ln_g[i, 0], ln_b[i, 0], alpha)
        xp = _mlp_ln(xp, w_up_b[i], w_down_b[i], ln_g[i, 1], ln_b[i, 1], alpha)
        xs = _mlp_ln(xs, w_up_b[i], w_down_b[i], ln_g[i, 1], ln_b[i, 1], alpha)
    return (xp.reshape(batch, seq, d), xs.reshape(db, 1, d),
            jnp.stack(sb_kp), jnp.stack(sb_vp), jnp.stack(sb_ks), jnp.stack(sb_vs),
            jnp.stack(df_kp), jnp.stack(df_vp), jnp.stack(df_ks), jnp.stack(df_vs))
```

```python
import functools
import math

import numpy as np
import jax
import jax.numpy as jnp
from jax import lax
from jax.experimental import pallas as pl
from jax.experimental.pallas import tpu as pltpu

HEAD_DIM = 64
PAGE_SIZE = 128
REL_BUCKETS = 32
REL_MAX_DIST = 128
ATT_SCALE = HEAD_DIM ** -0.5
NEG_BIG = -1e30
LN_EPS = 1e-5
LOG2E = math.log2(math.e)

LANES = 128
MXU_DIM = 256
VMEM_LIMIT_BYTES = 48 * 1024 * 1024

ATT_BLOCK = 2 * MXU_DIM
ROW_BLOCK = 512
FF_CHUNK = 1024
DEC_HEADS = 16
DEC_PAGES_PER_STEP = 8

F32 = jnp.float32
BF16 = jnp.bfloat16


def _params(*semantics):
    return pltpu.CompilerParams(dimension_semantics=semantics,
                                vmem_limit_bytes=VMEM_LIMIT_BYTES)


def _dot(a, b):
    return jnp.dot(a, b, preferred_element_type=F32)


def _dot_nt(a, b):
    return lax.dot_general(a, b, (((1,), (1,)), ((), ())), preferred_element_type=F32)


def _suffix_matrix_np(n):
    return np.where(np.arange(n)[:, None] > np.arange(n)[None, :], -1.0, 0.0)


def _suffix_matrix(n):
    r = lax.broadcasted_iota(jnp.int32, (n, n), 0)
    c = lax.broadcasted_iota(jnp.int32, (n, n), 1)
    return jnp.where(r > c, -1.0, 0.0).astype(BF16)


def _split_halves(q2):
    qf = q2.astype(F32)
    first = lax.broadcasted_iota(jnp.int32, qf.shape, 1) < HEAD_DIM
    return jnp.where(first, qf, 0.0).astype(BF16), jnp.where(first, 0.0, qf).astype(BF16)


def _bf16_round(x):
    return x.astype(BF16).astype(F32)


def _rep(col):
    return jnp.broadcast_to(col, (col.shape[0], LANES))


def _lane_blocks(x):
    return [x[:, cb * LANES:(cb + 1) * LANES] for cb in range(x.shape[1] // LANES)]


def _softplus2(z):
    neg_abs = pltpu.bitcast(pltpu.bitcast(z, jnp.uint32) | jnp.uint32(0x80000000), F32)
    return jnp.maximum(z, 0.0) + jnp.log2(1.0 + jnp.exp2(neg_abs))


def _proj_kernel(x_ref, w_ref, q_ref, k_ref, v_ref, kb_ref, vb_ref, *, q_scale):
    d = x_ref.shape[1]
    x = x_ref[...].astype(BF16)
    q = _dot(x, w_ref[:, 0:d])
    q_ref[...] = (q * q_scale).astype(BF16)
    k = _dot(x, w_ref[:, d:2 * d])
    k_ref[...] = k
    kb_ref[...] = k.astype(BF16)
    v = _dot(x, w_ref[:, 2 * d:3 * d])
    v_ref[...] = v
    vb_ref[...] = v.astype(BF16)


def _qkv_proj(x, w_bf16, q_scale):
    m, d = x.shape
    bm = min(ROW_BLOCK, m)
    row = pl.BlockSpec((bm, d), lambda i: (i, 0))
    return pl.pallas_call(
        functools.partial(_proj_kernel, q_scale=q_scale),
        grid=(m // bm,),
        in_specs=[row, pl.BlockSpec((d, 3 * d), lambda i: (0, 0))],
        out_specs=[row, row, row, row, row],
        out_shape=[jax.ShapeDtypeStruct((m, d), BF16),
                   jax.ShapeDtypeStruct((m, d), F32),
                   jax.ShapeDtypeStruct((m, d), F32),
                   jax.ShapeDtypeStruct((m, d), BF16),
                   jax.ShapeDtypeStruct((m, d), BF16)],
        compiler_params=_params("parallel"),
        name="qkv_proj",
    )(x, w_bf16)


def _layernorm(xf, g, b):
    mu = jnp.mean(xf, axis=-1, keepdims=True)
    xc = xf - mu
    var = jnp.mean(xc * xc, axis=-1, keepdims=True)
    return xc * lax.rsqrt(var + LN_EPS) * g + b


def _outproj_ln_kernel(o_ref, x_ref, w_ref, g_ref, b_ref, y_ref, *, alpha):
    m = _dot(o_ref[...], w_ref[...])
    y_ref[...] = _layernorm(alpha * x_ref[...] + m, g_ref[...], b_ref[...])


def _outproj_ln(o_bf16, x, w_bf16, g, b, alpha):
    m, d = x.shape
    bm = min(ROW_BLOCK, m)
    row = pl.BlockSpec((bm, d), lambda i: (i, 0))
    vec = pl.BlockSpec((1, d), lambda i: (0, 0))
    return pl.pallas_call(
        functools.partial(_outproj_ln_kernel, alpha=alpha),
        grid=(m // bm,),
        in_specs=[row, row, pl.BlockSpec((d, d), lambda i: (0, 0)), vec, vec],
        out_specs=row,
        out_shape=jax.ShapeDtypeStruct((m, d), F32),
        compiler_params=_params("parallel"),
        name="outproj_ln",
    )(o_bf16, x, w_bf16, g.reshape(1, d), b.reshape(1, d))


def _mlp_ln_kernel(x_ref, wu_ref, wd_ref, g_ref, b_ref, y_ref, acc_ref, *, alpha):
    c = pl.program_id(1)
    x = x_ref[...]
    h = jnp.maximum(_dot(x.astype(BF16), wu_ref[...]), 0.0)
    part = _dot((h * h).astype(BF16), wd_ref[...])

    @pl.when(c == 0)
    def _():
        acc_ref[...] = part

    @pl.when(c > 0)
    def _():
        acc_ref[...] += part

    @pl.when(c == pl.num_programs(1) - 1)
    def _():
        y_ref[...] = _layernorm(alpha * x + acc_ref[...], g_ref[...], b_ref[...])


def _mlp_ln(x, wu_bf16, wd_bf16, g, b, alpha):
    m, d = x.shape
    dff = wu_bf16.shape[1]
    bm = min(ROW_BLOCK, m)
    fc = min(FF_CHUNK, dff)
    row = pl.BlockSpec((bm, d), lambda i, c: (i, 0))
    vec = pl.BlockSpec((1, d), lambda i, c: (0, 0))
    return pl.pallas_call(
        functools.partial(_mlp_ln_kernel, alpha=alpha),
        grid=(m // bm, dff // fc),
        in_specs=[row,
                  pl.BlockSpec((d, fc), lambda i, c: (0, c)),
                  pl.BlockSpec((fc, d), lambda i, c: (c, 0)),
                  vec, vec],
        out_specs=row,
        out_shape=jax.ShapeDtypeStruct((m, d), F32),
        scratch_shapes=[pltpu.VMEM((bm, d), F32)],
        compiler_params=_params("parallel", "arbitrary"),
        name="mlp_ln",
    )(x, wu_bf16, wd_bf16, g.reshape(1, d), b.reshape(1, d))


def _sb_scores(qh, kb, mask, nls_ref, nls0_ref, logb_ref):
    z = _dot_nt(qh, kb)
    nls = _softplus2(z)
    logb = z - nls
    if mask is not None:
        nls = jnp.where(mask, nls, 0.0)
        logb = jnp.where(mask, logb, NEG_BIG)
    nls_ref[...] = nls.astype(BF16)
    for ch in range(nls.shape[1] // MXU_DIM):
        nls0_ref[ch] = _rep(_bf16_round(nls[:, ch * MXU_DIM:ch * MXU_DIM + 1]))
    logb_ref[...] = logb


def _sb_weights(nls_ref, nls0_ref, logb_ref, vb, u, acc_ref, c_ref):
    t = nls_ref.shape[0]
    n_chunks = nls_ref.shape[1] // MXU_DIM
    later = jnp.zeros((t, LANES), F32)
    a = [None] * n_chunks
    for ch in reversed(range(n_chunks)):
        cols = slice(ch * MXU_DIM, (ch + 1) * MXU_DIM)
        after = _dot(nls_ref[:, cols], u)
        blocks = []
        for lb in range(MXU_DIM // LANES):
            lanes = slice(lb * LANES, (lb + 1) * LANES)
            e = logb_ref[:, cols][:, lanes] + after[:, lanes]
            blocks.append(jnp.exp2(e if ch == n_chunks - 1 else e + later).astype(BF16))
        a[ch] = jnp.concatenate(blocks, axis=1)
        later = later + (_rep(after[:, 0:1]) - nls0_ref[ch])
    o = _dot(jnp.concatenate(a, axis=1), vb)
    c = c_ref[...]
    acc_ref[...] += jnp.exp2(c) * o
    c_ref[...] = c + later


def _sb_prompt_kernel(u_ref, q_ref, k_ref, v_ref, o_ref, qs_ref, nls_ref, nls0_ref, logb_ref, acc_ref, c_ref):
    t = q_ref.shape[1]
    i = pl.program_id(2)
    qs_ref[0], qs_ref[1] = _split_halves(q_ref[0])
    acc_ref[...] = jnp.zeros_like(acc_ref)
    c_ref[...] = jnp.zeros_like(c_ref)

    def tile_rows(ref, j):
        return ref[0, pl.ds(pl.multiple_of(j * t, t), t), :]

    def scores(j, slot, mask=None):
        kb = tile_rows(k_ref, j)
        for h in range(2):
            _sb_scores(qs_ref[h], kb, mask, nls_ref.at[slot, h], nls0_ref.at[slot, h], logb_ref.at[slot, h])

    def weights(j, slot):
        vb = tile_rows(v_ref, j)
        for h in range(2):
            _sb_weights(nls_ref.at[slot, h], nls0_ref.at[slot, h], logb_ref.at[slot, h], vb, u_ref[...],
                        acc_ref.at[h], c_ref.at[h])

    r = lax.broadcasted_iota(jnp.int32, (t, t), 0)
    c = lax.broadcasted_iota(jnp.int32, (t, t), 1)
    scores(i, 0, c < r)

    def body(s, _):
        j = i - 1 - 2 * s
        scores(j, 1)
        weights(j + 1, 0)
        scores(j - 1, 0)
        weights(j, 1)
        return 0

    lax.fori_loop(0, i // 2, body, 0)

    @pl.when(i % 2 == 1)
    def _():
        scores(0, 1)
        weights(1, 0)
        weights(0, 1)

    @pl.when(i % 2 == 0)
    def _():
        weights(0, 0)

    lane_o = lax.broadcasted_iota(jnp.int32, (t, LANES), 1)
    o_ref[0] = jnp.where(lane_o < HEAD_DIM, acc_ref[0], acc_ref[1]).astype(o_ref.dtype)


def _sb_prompt_attention(q, k, v, batch, seq):
    d = q.shape[1]
    t = min(ATT_BLOCK, seq)
    q3, k3, v3 = (a.reshape(batch, seq, d) for a in (q, k, v))
    kv_spec = pl.BlockSpec((1, seq, LANES), lambda b, h, i: (b, 0, h))
    q_spec = pl.BlockSpec((1, t, LANES), lambda b, h, i: (b, i, h))
    out = pl.pallas_call(
        _sb_prompt_kernel,
        grid=(batch, d // LANES, seq // t),
        in_specs=[pl.BlockSpec((MXU_DIM, MXU_DIM), lambda b, h, i: (0, 0)), q_spec, kv_spec, kv_spec],
        out_specs=q_spec,
        out_shape=jax.ShapeDtypeStruct((batch, seq, d), BF16),
        scratch_shapes=[pltpu.VMEM((2, t, LANES), BF16),
                        pltpu.VMEM((2, 2, t, t), BF16),
                        pltpu.VMEM((2, 2, t // MXU_DIM, t, LANES), F32),
                        pltpu.VMEM((2, 2, t, t), F32),
                        pltpu.VMEM((2, t, LANES), F32),
                        pltpu.VMEM((2, t, LANES), F32)],
        compiler_params=_params("parallel", "parallel", "arbitrary"),
        name="sb_prompt",
    )(jnp.asarray(_suffix_matrix_np(MXU_DIM), BF16), q3, k3, v3)
    return out.reshape(batch * seq, d)


def _t5_bucket_np(dist):
    n = np.maximum(dist, 0).astype(np.int32)
    max_exact = REL_BUCKETS // 2
    nf = np.maximum(n, 1).astype(np.float32)
    large = max_exact + (np.log(nf / np.float32(max_exact)) / np.float32(math.log(REL_MAX_DIST / max_exact))
                         * np.float32(REL_BUCKETS - max_exact)).astype(np.int32)
    large = np.minimum(large, REL_BUCKETS - 1)
    return np.where(n < max_exact, n, large).astype(np.int32)


def _bias_table_kernel(rel_ref, bsub_ref, bdiag_ref, bdec_ref, tab_ref, dec_ref, new_ref):
    h = pl.program_id(0)
    far = rel_ref[REL_BUCKETS - 1, h]

    def gather(bucket):
        out = jnp.zeros(bucket.shape, F32)
        for b in range(REL_BUCKETS):
            out = jnp.where(bucket == b, rel_ref[b, h], out)
        return (out - far) * LOG2E

    tab_ref[0, 0] = gather(bsub_ref[...])
    tab_ref[0, 1] = gather(bdiag_ref[...])
    dec_ref[0] = gather(bdec_ref[...])
    new_ref[0] = jnp.zeros(new_ref.shape[1:], F32) + (rel_ref[0, h] - far) * LOG2E


def _bias_tables(rel_bias, t):
    assert t >= REL_MAX_DIST and PAGE_SIZE >= REL_MAX_DIST
    n_heads = rel_bias.shape[1]
    r = np.arange(t)[:, None]
    c = np.arange(t)[None, :]
    b_sub = _t5_bucket_np(r - c + t)
    b_diag = _t5_bucket_np(r - c)
    b_dec = _t5_bucket_np(PAGE_SIZE - np.arange(PAGE_SIZE))[None, :]
    full = lambda shape: pl.BlockSpec(shape, lambda h: (0,) * len(shape))
    return pl.pallas_call(
        _bias_table_kernel,
        grid=(n_heads,),
        in_specs=[pl.BlockSpec(memory_space=pltpu.SMEM), full((t, t)), full((t, t)), full((1, PAGE_SIZE))],
        out_specs=[pl.BlockSpec((1, 2, t, t), lambda h: (h, 0, 0, 0)),
                   pl.BlockSpec((1, 1, PAGE_SIZE), lambda h: (h, 0, 0)),
                   pl.BlockSpec((1, 1, LANES), lambda h: (h, 0, 0))],
        out_shape=[jax.ShapeDtypeStruct((n_heads, 2, t, t), F32),
                   jax.ShapeDtypeStruct((n_heads, 1, PAGE_SIZE), F32),
                   jax.ShapeDtypeStruct((n_heads, 1, LANES), F32)],
        compiler_params=_params("arbitrary"),
        name="t5_bias_tables",
    )(rel_bias, jnp.asarray(b_sub), jnp.asarray(b_diag), jnp.asarray(b_dec))


def _df_lambda(lv_ref, lam_init):
    lv = lv_ref[...]
    a = jnp.sum(lv[0:1, :] * lv[1:2, :], axis=1, keepdims=True)
    b = jnp.sum(lv[2:3, :] * lv[3:4, :], axis=1, keepdims=True)
    return jnp.exp(a) - jnp.exp(b) + lam_init


def _df_prompt_kernel(lv_ref, g_ref, q_ref, k_ref, v_ref, tab_ref, o_ref,
                      qs_ref, s_ref, rmax_ref, m_ref, l_ref, acc_ref, *, lam_init):
    t = q_ref.shape[1]
    i = pl.program_id(2)
    qs_ref[0], qs_ref[1] = _split_halves(q_ref[0])
    m_ref[...] = jnp.full_like(m_ref, NEG_BIG)
    l_ref[...] = jnp.zeros_like(l_ref)
    acc_ref[...] = jnp.zeros_like(acc_ref)

    def tile_rows(ref, j):
        return ref[0, pl.ds(pl.multiple_of(j * t, t), t), :]

    def scores(j, slot, bias=None, mask=None):
        kb = tile_rows(k_ref, j)
        for c in range(2):
            s = _dot_nt(qs_ref[c], kb)
            if bias is not None:
                s = s + bias
            if mask is not None:
                s = jnp.where(mask, s, NEG_BIG)
            s_ref[slot, c] = s
            rmax_ref[slot, c] = _rep(jnp.max(s, axis=1, keepdims=True))

    def update(j, slot):
        vb = tile_rows(v_ref, j)
        for c in range(2):
            m = m_ref[c]
            m_new = jnp.maximum(m, rmax_ref[slot, c])
            alpha = jnp.exp2(m - m_new)
            p = [jnp.exp2(sb - m_new) for sb in _lane_blocks(s_ref[slot, c])]
            psum = p[0]
            for pb in p[1:]:
                psum = psum + pb
            l_ref[c] = alpha * l_ref[c] + _rep(jnp.sum(psum, axis=1, keepdims=True))
            pv = _dot(jnp.concatenate([pb.astype(BF16) for pb in p], axis=1), vb)
            acc_ref[c] = alpha * acc_ref[c] + pv
            m_ref[c] = m_new

    r = lax.broadcasted_iota(jnp.int32, (t, t), 0)
    c = lax.broadcasted_iota(jnp.int32, (t, t), 1)
    scores(i, 0, tab_ref[0, 1], c <= r)

    @pl.when(i > 0)
    def _():
        scores(i - 1, 1, tab_ref[0, 0])
        update(i, 0)

    def body(s, _):
        j = i - 2 - 2 * s
        scores(j, 0)
        update(j + 1, 1)
        scores(j - 1, 1)
        update(j, 0)
        return 0

    n_far = jnp.maximum(i - 1, 0)
    lax.fori_loop(0, n_far // 2, body, 0)

    @pl.when(n_far % 2 == 1)
    def _():
        scores(0, 0)
        update(1, 1)
        update(0, 0)

    @pl.when((n_far % 2 == 0) & (i > 0))
    def _():
        update(0, 1)

    @pl.when(i == 0)
    def _():
        update(0, 0)

    lam = _df_lambda(lv_ref, lam_init)
    o = acc_ref[0] / l_ref[0] - lam * (acc_ref[1] / l_ref[1])
    o = o * lax.rsqrt(jnp.mean(o * o, axis=1, keepdims=True) + 1e-5)
    o_ref[0] = (o * g_ref[...] * (1.0 - lam_init)).astype(o_ref.dtype)


def _df_prompt_attention(q, k, v, tab, lam_vec, subln_g, batch, seq, lam_init):
    d = q.shape[1]
    t = tab.shape[2]
    q3, k3, v3 = (a.reshape(batch, seq, d) for a in (q, k, v))
    kv_spec = pl.BlockSpec((1, seq, LANES), lambda b, h, i: (b, 0, h))
    q_spec = pl.BlockSpec((1, t, LANES), lambda b, h, i: (b, i, h))
    out = pl.pallas_call(
        functools.partial(_df_prompt_kernel, lam_init=lam_init),
        grid=(batch, d // LANES, seq // t),
        in_specs=[pl.BlockSpec(lam_vec.shape, lambda b, h, i: (0, 0)),
                  pl.BlockSpec((1, LANES), lambda b, h, i: (0, 0)),
                  q_spec, kv_spec, kv_spec,
                  pl.BlockSpec((1, 2, t, t), lambda b, h, i: (h, 0, 0, 0))],
        out_specs=q_spec,
        out_shape=jax.ShapeDtypeStruct((batch, seq, d), BF16),
        scratch_shapes=[pltpu.VMEM((2, t, LANES), BF16),
                        pltpu.VMEM((2, 2, t, t), F32),
                        pltpu.VMEM((2, 2, t, LANES), F32),
                        pltpu.VMEM((2, t, LANES), F32),
                        pltpu.VMEM((2, t, LANES), F32),
                        pltpu.VMEM((2, t, LANES), F32)],
        compiler_params=_params("parallel", "parallel", "arbitrary"),
        name="df_prompt",
    )(lam_vec, subln_g.reshape(1, LANES), q3, k3, v3, tab)
    return out.reshape(batch * seq, d)


def _head_rows(q_row, col_of_row):
    d = q_row.shape[1]
    q = jnp.broadcast_to(q_row.astype(F32), (DEC_HEADS, d))
    row = lax.broadcasted_iota(jnp.int32, (DEC_HEADS, d), 0)
    col = lax.broadcasted_iota(jnp.int32, (DEC_HEADS, d), 1)
    start = col_of_row(row)
    return jnp.where((col >= start) & (col < start + HEAD_DIM), q, 0.0).astype(BF16)


def _sb_decode_kernel(pt_ref, q_ref, *refs):
    del pt_ref
    npg = (len(refs) - 3) // 2
    kt_refs, vt_refs = refs[:npg], refs[npg:2 * npg]
    o_ref, acc_ref, c_ref = refs[2 * npg:]
    s = pl.program_id(1)

    @pl.when(s == 0)
    def _():
        acc_ref[...] = jnp.zeros_like(acc_ref)
        c_ref[...] = jnp.zeros_like(c_ref)

    qrows = _head_rows(q_ref[0], lambda r: r * HEAD_DIM)
    u = _suffix_matrix(PAGE_SIZE)
    cc = c_ref[...]
    for p in range(npg):
        z = _dot(qrows, kt_refs[p][0].astype(BF16))
        nls = _softplus2(z)
        after = _dot(nls.astype(BF16), u)
        a = jnp.exp2(z - nls + after + cc)
        for h in range(DEC_HEADS):
            rows = slice(h * HEAD_DIM, (h + 1) * HEAD_DIM)
            acc_ref[rows, :] += vt_refs[p][0, rows, :] * a[h:h + 1, :]
        cc = cc + after[:, 0:1] - _bf16_round(nls[:, 0:1])
    c_ref[...] = cc

    @pl.when(s == pl.num_programs(1) - 1)
    def _():
        o_ref[0] = jnp.sum(acc_ref[...].T, axis=0, keepdims=True).astype(o_ref.dtype)


def _page_specs(n_pages, npg, block, reverse, first_page):
    specs = []
    for p in range(npg):
        if reverse:
            idx = lambda b, s, pt, p=p: (first_page + pt[b, n_pages - 1 - (s * npg + p)], 0, 0)
        else:
            idx = lambda b, s, pt, p=p: (first_page + pt[b, s * npg + p], 0, 0)
        specs.append(pl.BlockSpec((1,) + block, idx))
    return specs


def _sb_decode_attention(q, cache_kt, cache_vt, page_table, first_page):
    db, d = q.shape
    n_pages = page_table.shape[1]
    npg = min(DEC_PAGES_PER_STEP, n_pages)
    row = pl.BlockSpec((1, 1, d), lambda b, s, pt: (b, 0, 0))
    pages = lambda: _page_specs(n_pages, npg, (d, PAGE_SIZE), True, first_page)
    out = pl.pallas_call(
        _sb_decode_kernel,
        grid_spec=pltpu.PrefetchScalarGridSpec(
            num_scalar_prefetch=1,
            grid=(db, n_pages // npg),
            in_specs=[row] + pages() + pages(),
            out_specs=row,
            scratch_shapes=[pltpu.VMEM((d, PAGE_SIZE), F32), pltpu.VMEM((DEC_HEADS, 1), F32)]),
        out_shape=jax.ShapeDtypeStruct((db, 1, d), BF16),
        compiler_params=_params("parallel", "arbitrary"),
        name="sb_decode",
    )(page_table, q.reshape(db, 1, d), *([cache_kt] * npg), *([cache_vt] * npg))
    return out.reshape(db, d)


def _df_decode_kernel(pt_ref, lv_ref, g_ref, dec_ref, new_ref, spread_ref, q_ref, kn_ref, vn_ref, *refs, lam_init):
    del pt_ref
    npg = (len(refs) - 4) // 2
    kt_refs, v_refs = refs[:npg], refs[npg:2 * npg]
    o_ref, acc_ref, m_ref, l_ref = refs[2 * npg:]
    s = pl.program_id(1)
    last = s == pl.num_programs(1) - 1
    n_heads = DEC_HEADS // 2

    @pl.when(s == 0)
    def _():
        acc_ref[...] = jnp.zeros_like(acc_ref)
        m_ref[...] = jnp.full_like(m_ref, NEG_BIG)
        l_ref[...] = jnp.zeros_like(l_ref)

    qrows = _head_rows(q_ref[0], lambda r: (r % n_heads) * LANES + (r // n_heads) * HEAD_DIM)
    scores = []
    for p in range(npg):
        sp = _dot(qrows, kt_refs[p][0].astype(BF16))
        if p == npg - 1:
            sp = sp + jnp.where(last, dec_ref[...], 0.0)
        scores.append(sp)
    m_old = m_ref[...]
    m_new = m_old
    for sp in scores:
        m_new = jnp.maximum(m_new, jnp.max(sp, axis=1, keepdims=True))
    alpha = jnp.exp2(m_old - m_new)
    l = alpha * l_ref[...]
    acc = alpha * acc_ref[...]
    row = lax.broadcasted_iota(jnp.int32, (DEC_HEADS, PAGE_SIZE * n_heads), 0)
    lane = lax.broadcasted_iota(jnp.int32, (DEC_HEADS, PAGE_SIZE * n_heads), 1)
    own_head = lane % n_heads == row % n_heads
    for p, sp in enumerate(scores):
        pr = jnp.exp2(sp - m_new)
        l = l + jnp.sum(pr, axis=1, keepdims=True)
        spread = _dot(pr.astype(BF16), spread_ref[...])
        spread = jnp.where(own_head, spread, 0.0).astype(BF16)
        acc = acc + _dot(spread, v_refs[p][0].astype(BF16))
    m_ref[...] = m_new
    l_ref[...] = l
    acc_ref[...] = acc

    @pl.when(last)
    def _():
        kn = kn_ref[0].astype(F32)
        s_new = jnp.sum(qrows.astype(F32) * kn, axis=1, keepdims=True) + new_ref[...][:, 0:1]
        m_fin = jnp.maximum(m_new, s_new)
        a_old = jnp.exp2(m_new - m_fin)
        p_new = jnp.exp2(s_new - m_fin)
        l_fin = a_old * l + p_new
        vn = vn_ref[0]
        acc_fin = a_old * acc + p_new * jnp.concatenate([vn, vn], axis=0)
        norm = acc_fin / l_fin
        lam = _df_lambda(lv_ref, lam_init)
        o = norm[0:n_heads] - lam * norm[n_heads:DEC_HEADS]
        o = o * lax.rsqrt(jnp.mean(o * o, axis=1, keepdims=True) + 1e-5)
        o_ref[0] = (o * g_ref[...] * (1.0 - lam_init)).astype(o_ref.dtype)


def _df_decode_attention(q, k_new, v_new, cache_kt, cache_v, page_table, first_page, dec_bias, new_bias,
                         lam_vec, subln_g, lam_init):
    db, d = q.shape
    n_pages = page_table.shape[1]
    npg = min(DEC_PAGES_PER_STEP, n_pages)
    n_heads = d // LANES
    row = pl.BlockSpec((1, 1, d), lambda b, s, pt: (b, 0, 0))
    head_rows = pl.BlockSpec((1, n_heads, LANES), lambda b, s, pt: (b, 0, 0))
    const = lambda shape: pl.BlockSpec(shape, lambda b, s, pt: (0,) * len(shape))
    kt_pages = _page_specs(n_pages, npg, (d, PAGE_SIZE), False, first_page)
    v_pages = _page_specs(n_pages, npg, (PAGE_SIZE * n_heads, LANES), False, first_page)
    spread = np.repeat(np.eye(PAGE_SIZE), n_heads, axis=1)
    out = pl.pallas_call(
        functools.partial(_df_decode_kernel, lam_init=lam_init),
        grid_spec=pltpu.PrefetchScalarGridSpec(
            num_scalar_prefetch=1,
            grid=(db, n_pages // npg),
            in_specs=[const(lam_vec.shape), const((1, LANES)), const((DEC_HEADS, PAGE_SIZE)),
                      const((DEC_HEADS, LANES)), const(spread.shape), row, row, head_rows] + kt_pages + v_pages,
            out_specs=head_rows,
            scratch_shapes=[pltpu.VMEM((DEC_HEADS, LANES), F32), pltpu.VMEM((DEC_HEADS, 1), F32),
                            pltpu.VMEM((DEC_HEADS, 1), F32)]),
        out_shape=jax.ShapeDtypeStruct((db, n_heads, LANES), BF16),
        compiler_params=_params("parallel", "arbitrary"),
        name="df_decode",
    )(page_table, lam_vec, subln_g.reshape(1, LANES), dec_bias, new_bias, jnp.asarray(spread, BF16),
      q.reshape(db, 1, d), k_new.reshape(db, 1, d), v_new.reshape(db, n_heads, LANES),
      *([cache_kt] * npg), *([cache_v] * npg))
    return out.reshape(db, d)


def kernel(x_prompt, x_sample, cache_sb_k, cache_sb_v, cache_df_k, cache_df_v, page_table,
           w_in, w_out, ln_g, ln_b, df_lambda, df_subln_g, rel_bias, w_up, w_down):
    batch, seq, d = x_prompt.shape
    db = x_sample.shape[0]
    depth = w_in.shape[0]
    n_sb_heads = d // HEAD_DIM
    n_df_heads = d // (2 * HEAD_DIM)
    alpha = (2 * depth) ** 0.25
    q_scale = ATT_SCALE * LOG2E
    t = min(ATT_BLOCK, seq)
    assert x_sample.shape[1] == 1 and d // HEAD_DIM == DEC_HEADS
    assert cache_sb_k.shape[2] == PAGE_SIZE and seq % t == 0 and t % MXU_DIM == 0

    xp = x_prompt.reshape(batch * seq, d)
    xs = x_sample.reshape(db, d)
    w_in_b, w_out_b = w_in.astype(BF16), w_out.astype(BF16)
    w_up_b, w_down_b = w_up.astype(BF16), w_down.astype(BF16)
    n_pool = cache_sb_k.shape[1]
    feature_major = lambda cache: jnp.moveaxis(cache, 2, -1).reshape(-1, d, PAGE_SIZE)
    df_v_pages = cache_df_v.reshape(-1, PAGE_SIZE * n_df_heads, 2 * HEAD_DIM)

    sb_kp, sb_vp, sb_ks, sb_vs = [], [], [], []
    df_kp, df_vp, df_ks, df_vs = [], [], [], []
    for i in range(depth):
        j = i // 2
        qp, kp, vp, kpb, vpb = _qkv_proj(xp, w_in_b[i], q_scale)
        qs, ks_, vs_, _, _ = _qkv_proj(xs, w_in_b[i], q_scale)
        if i % 2 == 0:
            op = _sb_prompt_attention(qp, kpb, vpb, batch, seq)
            os_ = _sb_decode_attention(qs, feature_major(cache_sb_k), feature_major(cache_sb_v),
                                       page_table, j * n_pool)
            sb_kp.append(kp.reshape(batch, seq, n_sb_heads, HEAD_DIM))
            sb_vp.append(vp.reshape(batch, seq, n_sb_heads, HEAD_DIM))
            sb_ks.append(ks_.reshape(db, 1, n_sb_heads, HEAD_DIM))
            sb_vs.append(vs_.reshape(db, 1, n_sb_heads, HEAD_DIM))
        else:
            lam_init = 0.8 - 0.6 * math.exp(-0.3 * i)
            tab, dec_bias, new_bias = _bias_tables(rel_bias, t)
            dec_bias = jnp.tile(dec_bias.reshape(n_df_heads, PAGE_SIZE), (2, 1))
            new_bias = jnp.tile(new_bias.reshape(n_df_heads, LANES), (2, 1))
            op = _df_prompt_attention(qp, kpb, vpb, tab, df_lambda[j], df_subln_g[j], batch, seq, lam_init)
            os_ = _df_decode_attention(qs, ks_, vs_, feature_major(cache_df_k), df_v_pages, page_table, j * n_pool,
                                       dec_bias, new_bias, df_lambda[j], df_subln_g[j], lam_init)
            df_kp.append(kp.reshape(batch, seq, n_df_heads, 2, HEAD_DIM))
            df_vp.append(vp.reshape(batch, seq, n_df_heads, 2 * HEAD_DIM))
            df_ks.append(ks_.reshape(db, 1, n_df_heads, 2, HEAD_DIM))
            df_vs.append(vs_.reshape(db, 1, n_df_heads, 2 * HEAD_DIM))
        xp = _outproj_ln(op, xp, w_out_b[i], ln_g[i, 0], ln_b[i, 0], alpha)
        xs = _outproj_ln(os_, xs, w_out_b[i], ln_g[i, 0], ln_b[i, 0], alpha)
        xp = _mlp_ln(xp, w_up_b[i], w_down_b[i], ln_g[i, 1], ln_b[i, 1], alpha)
        xs = _mlp_ln(xs, w_up_b[i], w_down_b[i], ln_g[i, 1], ln_b[i, 1], alpha)
    return (xp.reshape(batch, seq, d), xs.reshape(db, 1, d),
            jnp.stack(sb_kp), jnp.stack(sb_vp), jnp.stack(sb_ks), jnp.stack(sb_vs),
            jnp.stack(df_kp), jnp.stack(df_vp), jnp.stack(df_ks), jnp.stack(df_vs))
```

```python
import functools
import math

import numpy as np
import jax
import jax.numpy as jnp
from jax import lax
from jax.experimental import pallas as pl
from jax.experimental.pallas import tpu as pltpu

HEAD_DIM = 64
PAGE_SIZE = 128
REL_BUCKETS = 32
REL_MAX_DIST = 128
ATT_SCALE = HEAD_DIM ** -0.5
NEG_BIG = -1e30
LN_EPS = 1e-5
LOG2E = math.log2(math.e)

LANES = 128
MXU_DIM = 256
VMEM_LIMIT_BYTES = 48 * 1024 * 1024

ATT_BLOCK = 2 * MXU_DIM
ROW_BLOCK = 512
FF_CHUNK = 1024
DEC_HEADS = 16
DEC_PAGES_PER_STEP = 16

F32 = jnp.float32
BF16 = jnp.bfloat16


def _params(*semantics):
    return pltpu.CompilerParams(dimension_semantics=semantics,
                                vmem_limit_bytes=VMEM_LIMIT_BYTES)


def _dot(a, b):
    return jnp.dot(a, b, preferred_element_type=F32)


def _dot_nt(a, b):
    return lax.dot_general(a, b, (((1,), (1,)), ((), ())), preferred_element_type=F32)


def _suffix_matrix_np(n):
    return np.where(np.arange(n)[:, None] > np.arange(n)[None, :], -1.0, 0.0)


def _suffix_matrix(n):
    r = lax.broadcasted_iota(jnp.int32, (n, n), 0)
    c = lax.broadcasted_iota(jnp.int32, (n, n), 1)
    return jnp.where(r > c, -1.0, 0.0).astype(BF16)


def _split_halves(q2):
    qf = q2.astype(F32)
    first = lax.broadcasted_iota(jnp.int32, qf.shape, 1) < HEAD_DIM
    return jnp.where(first, qf, 0.0).astype(BF16), jnp.where(first, 0.0, qf).astype(BF16)


def _bf16_round(x):
    return x.astype(BF16).astype(F32)


def _rep(col):
    return jnp.broadcast_to(col, (col.shape[0], LANES))


def _lane_blocks(x):
    return [x[:, cb * LANES:(cb + 1) * LANES] for cb in range(x.shape[1] // LANES)]


def _softplus2(z):
    neg_abs = pltpu.bitcast(pltpu.bitcast(z, jnp.uint32) | jnp.uint32(0x80000000), F32)
    return jnp.maximum(z, 0.0) + jnp.log2(1.0 + jnp.exp2(neg_abs))


def _proj_kernel(x_ref, w_ref, q_ref, k_ref, v_ref, *, q_scale):
    d = x_ref.shape[1]
    x = x_ref[...].astype(BF16)
    q_ref[...] = (_dot(x, w_ref[:, 0:d]) * q_scale).astype(BF16)
    k_ref[...] = _dot(x, w_ref[:, d:2 * d])
    v_ref[...] = _dot(x, w_ref[:, 2 * d:3 * d])


def _qkv_proj(x, w_bf16, q_scale):
    m, d = x.shape
    bm = min(ROW_BLOCK, m)
    row = pl.BlockSpec((bm, d), lambda i: (i, 0))
    return pl.pallas_call(
        functools.partial(_proj_kernel, q_scale=q_scale),
        grid=(m // bm,),
        in_specs=[row, pl.BlockSpec((d, 3 * d), lambda i: (0, 0))],
        out_specs=[row, row, row],
        out_shape=[jax.ShapeDtypeStruct((m, d), BF16),
                   jax.ShapeDtypeStruct((m, d), F32),
                   jax.ShapeDtypeStruct((m, d), F32)],
        compiler_params=_params("parallel"),
        name="qkv_proj",
    )(x, w_bf16)


def _proj_fm_kernel(x_ref, w_ref, q_ref, kt_ref, ktb_ref, v_ref, vb_ref, *, q_scale, v_feature_major):
    d = x_ref.shape[1]
    x = x_ref[...].astype(BF16)
    q_ref[...] = (_dot(x, w_ref[:, 0:d]) * q_scale).astype(BF16)
    kt = _dot(x, w_ref[:, d:2 * d]).T
    kt_ref[0] = kt
    ktb_ref[0] = kt.astype(BF16)
    v = _dot(x, w_ref[:, 2 * d:3 * d])
    vb_ref[...] = v.astype(BF16)
    if v_feature_major:
        v_ref[0] = v.T
    else:
        v_ref[...] = v


def _qkv_proj_fm(x, w_bf16, q_scale, batch, v_feature_major):
    m, d = x.shape
    seq = m // batch
    bm = min(ROW_BLOCK, seq)
    nb = seq // bm
    row = pl.BlockSpec((bm, d), lambda i: (i, 0))
    fm = pl.BlockSpec((1, d, bm), lambda i: (i // nb, 0, i % nb))
    fm_f32 = jax.ShapeDtypeStruct((batch, d, seq), F32)
    return pl.pallas_call(
        functools.partial(_proj_fm_kernel, q_scale=q_scale, v_feature_major=v_feature_major),
        grid=(m // bm,),
        in_specs=[row, pl.BlockSpec((d, 3 * d), lambda i: (0, 0))],
        out_specs=[row, fm, fm, fm if v_feature_major else row, row],
        out_shape=[jax.ShapeDtypeStruct((m, d), BF16),
                   fm_f32,
                   jax.ShapeDtypeStruct((batch, d, seq), BF16),
                   fm_f32 if v_feature_major else jax.ShapeDtypeStruct((m, d), F32),
                   jax.ShapeDtypeStruct((m, d), BF16)],
        compiler_params=_params("parallel"),
        name="qkv_proj_fm",
    )(x, w_bf16)


def _layernorm(xf, g, b):
    mu = jnp.mean(xf, axis=-1, keepdims=True)
    xc = xf - mu
    var = jnp.mean(xc * xc, axis=-1, keepdims=True)
    return xc * lax.rsqrt(var + LN_EPS) * g + b


def _outproj_ln_kernel(o_ref, x_ref, w_ref, g_ref, b_ref, y_ref, *, alpha):
    m = _dot(o_ref[...], w_ref[...])
    y_ref[...] = _layernorm(alpha * x_ref[...] + m, g_ref[...], b_ref[...])


def _outproj_ln(o_bf16, x, w_bf16, g, b, alpha):
    m, d = x.shape
    bm = min(ROW_BLOCK, m)
    row = pl.BlockSpec((bm, d), lambda i: (i, 0))
    vec = pl.BlockSpec((1, d), lambda i: (0, 0))
    return pl.pallas_call(
        functools.partial(_outproj_ln_kernel, alpha=alpha),
        grid=(m // bm,),
        in_specs=[row, row, pl.BlockSpec((d, d), lambda i: (0, 0)), vec, vec],
        out_specs=row,
        out_shape=jax.ShapeDtypeStruct((m, d), F32),
        compiler_params=_params("parallel"),
        name="outproj_ln",
    )(o_bf16, x, w_bf16, g.reshape(1, d), b.reshape(1, d))


def _mlp_ln_kernel(x_ref, wu_ref, wd_ref, g_ref, b_ref, y_ref, acc_ref, *, alpha):
    c = pl.program_id(1)
    x = x_ref[...]
    h = jnp.maximum(_dot(x.astype(BF16), wu_ref[...]), 0.0)
    part = _dot((h * h).astype(BF16), wd_ref[...])

    @pl.when(c == 0)
    def _():
        acc_ref[...] = part

    @pl.when(c > 0)
    def _():
        acc_ref[...] += part

    @pl.when(c == pl.num_programs(1) - 1)
    def _():
        y_ref[...] = _layernorm(alpha * x + acc_ref[...], g_ref[...], b_ref[...])


def _mlp_ln(x, wu_bf16, wd_bf16, g, b, alpha):
    m, d = x.shape
    dff = wu_bf16.shape[1]
    bm = min(ROW_BLOCK, m)
    fc = min(FF_CHUNK, dff)
    row = pl.BlockSpec((bm, d), lambda i, c: (i, 0))
    vec = pl.BlockSpec((1, d), lambda i, c: (0, 0))
    return pl.pallas_call(
        functools.partial(_mlp_ln_kernel, alpha=alpha),
        grid=(m // bm, dff // fc),
        in_specs=[row,
                  pl.BlockSpec((d, fc), lambda i, c: (0, c)),
                  pl.BlockSpec((fc, d), lambda i, c: (c, 0)),
                  vec, vec],
        out_specs=row,
        out_shape=jax.ShapeDtypeStruct((m, d), F32),
        scratch_shapes=[pltpu.VMEM((bm, d), F32)],
        compiler_params=_params("parallel", "arbitrary"),
        name="mlp_ln",
    )(x, wu_bf16, wd_bf16, g.reshape(1, d), b.reshape(1, d))


def _sb_scores(qh, kt, mask, nls_ref, nls0_ref, logb_ref):
    z = _dot(qh, kt)
    nls = _softplus2(z)
    logb = z - nls
    if mask is not None:
        nls = jnp.where(mask, nls, 0.0)
        logb = jnp.where(mask, logb, NEG_BIG)
    nls_ref[...] = nls.astype(BF16)
    for ch in range(nls.shape[1] // MXU_DIM):
        nls0_ref[ch] = _rep(_bf16_round(nls[:, ch * MXU_DIM:ch * MXU_DIM + 1]))
    logb_ref[...] = logb


def _sb_weights(nls_ref, nls0_ref, logb_ref, vb, u, acc_ref, c_ref):
    t = nls_ref.shape[0]
    n_chunks = nls_ref.shape[1] // MXU_DIM
    later = jnp.zeros((t, LANES), F32)
    a = [None] * n_chunks
    for ch in reversed(range(n_chunks)):
        cols = slice(ch * MXU_DIM, (ch + 1) * MXU_DIM)
        after = _dot(nls_ref[:, cols], u)
        blocks = []
        for lb in range(MXU_DIM // LANES):
            lanes = slice(lb * LANES, (lb + 1) * LANES)
            e = logb_ref[:, cols][:, lanes] + after[:, lanes]
            blocks.append(jnp.exp2(e if ch == n_chunks - 1 else e + later).astype(BF16))
        a[ch] = jnp.concatenate(blocks, axis=1)
        later = later + (_rep(after[:, 0:1]) - nls0_ref[ch])
    o = _dot(jnp.concatenate(a, axis=1), vb)
    c = c_ref[...]
    acc_ref[...] += jnp.exp2(c) * o
    c_ref[...] = c + later


def _sb_prompt_kernel(u_ref, q_ref, kt_ref, v_ref, o_ref, qs_ref, nls_ref, nls0_ref, logb_ref, acc_ref, c_ref):
    t = q_ref.shape[1]
    i = pl.program_id(2)
    qs_ref[0], qs_ref[1] = _split_halves(q_ref[0])
    acc_ref[...] = jnp.zeros_like(acc_ref)
    c_ref[...] = jnp.zeros_like(c_ref)

    def scores(j, slot, mask=None):
        kt = kt_ref[0, :, pl.ds(pl.multiple_of(j * t, t), t)]
        for h in range(2):
            _sb_scores(qs_ref[h], kt, mask, nls_ref.at[slot, h], nls0_ref.at[slot, h], logb_ref.at[slot, h])

    def weights(j, slot):
        vb = v_ref[0, pl.ds(pl.multiple_of(j * t, t), t), :]
        for h in range(2):
            _sb_weights(nls_ref.at[slot, h], nls0_ref.at[slot, h], logb_ref.at[slot, h], vb, u_ref[...],
                        acc_ref.at[h], c_ref.at[h])

    r = lax.broadcasted_iota(jnp.int32, (t, t), 0)
    c = lax.broadcasted_iota(jnp.int32, (t, t), 1)
    scores(i, 0, c < r)

    def body(s, _):
        j = i - 1 - 2 * s
        scores(j, 1)
        weights(j + 1, 0)
        scores(j - 1, 0)
        weights(j, 1)
        return 0

    lax.fori_loop(0, i // 2, body, 0)

    @pl.when(i % 2 == 1)
    def _():
        scores(0, 1)
        weights(1, 0)
        weights(0, 1)

    @pl.when(i % 2 == 0)
    def _():
        weights(0, 0)

    lane_o = lax.broadcasted_iota(jnp.int32, (t, LANES), 1)
    o_ref[0] = jnp.where(lane_o < HEAD_DIM, acc_ref[0], acc_ref[1]).astype(o_ref.dtype)


def _sb_prompt_attention(q, kt, v, batch, seq):
    d = q.shape[1]
    t = min(ATT_BLOCK, seq)
    q3, v3 = (a.reshape(batch, seq, d) for a in (q, v))
    kt_spec = pl.BlockSpec((1, LANES, seq), lambda b, h, i: (b, h, 0))
    v_spec = pl.BlockSpec((1, seq, LANES), lambda b, h, i: (b, 0, h))
    q_spec = pl.BlockSpec((1, t, LANES), lambda b, h, i: (b, i, h))
    out = pl.pallas_call(
        _sb_prompt_kernel,
        grid=(batch, d // LANES, seq // t),
        in_specs=[pl.BlockSpec((MXU_DIM, MXU_DIM), lambda b, h, i: (0, 0)), q_spec, kt_spec, v_spec],
        out_specs=q_spec,
        out_shape=jax.ShapeDtypeStruct((batch, seq, d), BF16),
        scratch_shapes=[pltpu.VMEM((2, t, LANES), BF16),
                        pltpu.VMEM((2, 2, t, t), BF16),
                        pltpu.VMEM((2, 2, t // MXU_DIM, t, LANES), F32),
                        pltpu.VMEM((2, 2, t, t), F32),
                        pltpu.VMEM((2, t, LANES), F32),
                        pltpu.VMEM((2, t, LANES), F32)],
        compiler_params=_params("parallel", "parallel", "arbitrary"),
        name="sb_prompt",
    )(jnp.asarray(_suffix_matrix_np(MXU_DIM), BF16), q3, kt, v3)
    return out.reshape(batch * seq, d)


def _t5_bucket_np(dist):
    n = np.maximum(dist, 0).astype(np.int32)
    max_exact = REL_BUCKETS // 2
    nf = np.maximum(n, 1).astype(np.float32)
    large = max_exact + (np.log(nf / np.float32(max_exact)) / np.float32(math.log(REL_MAX_DIST / max_exact))
                         * np.float32(REL_BUCKETS - max_exact)).astype(np.int32)
    large = np.minimum(large, REL_BUCKETS - 1)
    return np.where(n < max_exact, n, large).astype(np.int32)


def _bias_table_kernel(rel_ref, bsub_ref, bdiag_ref, bdec_ref, tab_ref, dec_ref, new_ref):
    h = pl.program_id(0)
    far = rel_ref[REL_BUCKETS - 1, h]

    def gather(bucket):
        out = jnp.zeros(bucket.shape, F32)
        for b in range(REL_BUCKETS):
            out = jnp.where(bucket == b, rel_ref[b, h], out)
        return (out - far) * LOG2E

    tab_ref[0, 0] = gather(bsub_ref[...])
    tab_ref[0, 1] = gather(bdiag_ref[...])
    dec_ref[0] = gather(bdec_ref[...])
    new_ref[0] = jnp.zeros(new_ref.shape[1:], F32) + (rel_ref[0, h] - far) * LOG2E


def _bias_tables(rel_bias, t):
    assert t >= REL_MAX_DIST and PAGE_SIZE >= REL_MAX_DIST
    n_heads = rel_bias.shape[1]
    r = np.arange(t)[:, None]
    c = np.arange(t)[None, :]
    b_sub = _t5_bucket_np(r - c + t)
    b_diag = _t5_bucket_np(r - c)
    b_dec = _t5_bucket_np(PAGE_SIZE - np.arange(PAGE_SIZE))[None, :]
    full = lambda shape: pl.BlockSpec(shape, lambda h: (0,) * len(shape))
    return pl.pallas_call(
        _bias_table_kernel,
        grid=(n_heads,),
        in_specs=[pl.BlockSpec(memory_space=pltpu.SMEM), full((t, t)), full((t, t)), full((1, PAGE_SIZE))],
        out_specs=[pl.BlockSpec((1, 2, t, t), lambda h: (h, 0, 0, 0)),
                   pl.BlockSpec((1, 1, PAGE_SIZE), lambda h: (h, 0, 0)),
                   pl.BlockSpec((1, 1, LANES), lambda h: (h, 0, 0))],
        out_shape=[jax.ShapeDtypeStruct((n_heads, 2, t, t), F32),
                   jax.ShapeDtypeStruct((n_heads, 1, PAGE_SIZE), F32),
                   jax.ShapeDtypeStruct((n_heads, 1, LANES), F32)],
        compiler_params=_params("arbitrary"),
        name="t5_bias_tables",
    )(rel_bias, jnp.asarray(b_sub), jnp.asarray(b_diag), jnp.asarray(b_dec))


def _df_lambda(lv_ref, lam_init):
    lv = lv_ref[...]
    a = jnp.sum(lv[0:1, :] * lv[1:2, :], axis=1, keepdims=True)
    b = jnp.sum(lv[2:3, :] * lv[3:4, :], axis=1, keepdims=True)
    return jnp.exp(a) - jnp.exp(b) + lam_init


def _df_prompt_kernel(lv_ref, g_ref, q_ref, kt_ref, v_ref, tab_ref, o_ref,
                      qs_ref, s_ref, rmax_ref, m_ref, l_ref, acc_ref, *, lam_init):
    t = q_ref.shape[1]
    i = pl.program_id(2)
    qs_ref[0], qs_ref[1] = _split_halves(q_ref[0])
    m_ref[...] = jnp.full_like(m_ref, NEG_BIG)
    l_ref[...] = jnp.zeros_like(l_ref)
    acc_ref[...] = jnp.zeros_like(acc_ref)

    def scores(j, slot, bias=None, mask=None):
        kt = kt_ref[0, :, pl.ds(pl.multiple_of(j * t, t), t)]
        for c in range(2):
            s = _dot(qs_ref[c], kt)
            if bias is not None:
                s = s + bias
            if mask is not None:
                s = jnp.where(mask, s, NEG_BIG)
            s_ref[slot, c] = s
            rmax_ref[slot, c] = _rep(jnp.max(s, axis=1, keepdims=True))

    def update(j, slot):
        vb = v_ref[0, pl.ds(pl.multiple_of(j * t, t), t), :]
        for c in range(2):
            m = m_ref[c]
            m_new = jnp.maximum(m, rmax_ref[slot, c])
            alpha = jnp.exp2(m - m_new)
            p = [jnp.exp2(sb - m_new) for sb in _lane_blocks(s_ref[slot, c])]
            psum = p[0]
            for pb in p[1:]:
                psum = psum + pb
            l_ref[c] = alpha * l_ref[c] + _rep(jnp.sum(psum, axis=1, keepdims=True))
            pv = _dot(jnp.concatenate([pb.astype(BF16) for pb in p], axis=1), vb)
            acc_ref[c] = alpha * acc_ref[c] + pv
            m_ref[c] = m_new

    r = lax.broadcasted_iota(jnp.int32, (t, t), 0)
    c = lax.broadcasted_iota(jnp.int32, (t, t), 1)
    scores(i, 0, tab_ref[0, 1], c <= r)

    @pl.when(i > 0)
    def _():
        scores(i - 1, 1, tab_ref[0, 0])
        update(i, 0)

    def body(s, _):
        j = i - 2 - 2 * s
        scores(j, 0)
        update(j + 1, 1)
        scores(j - 1, 1)
        update(j, 0)
        return 0

    n_far = jnp.maximum(i - 1, 0)
    lax.fori_loop(0, n_far // 2, body, 0)

    @pl.when(n_far % 2 == 1)
    def _():
        scores(0, 0)
        update(1, 1)
        update(0, 0)

    @pl.when((n_far % 2 == 0) & (i > 0))
    def _():
        update(0, 1)

    @pl.when(i == 0)
    def _():
        update(0, 0)

    lam = _df_lambda(lv_ref, lam_init)
    o = acc_ref[0] / l_ref[0] - lam * (acc_ref[1] / l_ref[1])
    o = o * lax.rsqrt(jnp.mean(o * o, axis=1, keepdims=True) + 1e-5)
    o_ref[0] = (o * g_ref[...] * (1.0 - lam_init)).astype(o_ref.dtype)


def _df_prompt_attention(q, kt, v, tab, lam_vec, subln_g, batch, seq, lam_init):
    d = q.shape[1]
    t = tab.shape[2]
    q3, v3 = (a.reshape(batch, seq, d) for a in (q, v))
    kt_spec = pl.BlockSpec((1, LANES, seq), lambda b, h, i: (b, h, 0))
    v_spec = pl.BlockSpec((1, seq, LANES), lambda b, h, i: (b, 0, h))
    q_spec = pl.BlockSpec((1, t, LANES), lambda b, h, i: (b, i, h))
    out = pl.pallas_call(
        functools.partial(_df_prompt_kernel, lam_init=lam_init),
        grid=(batch, d // LANES, seq // t),
        in_specs=[pl.BlockSpec(lam_vec.shape, lambda b, h, i: (0, 0)),
                  pl.BlockSpec((1, LANES), lambda b, h, i: (0, 0)),
                  q_spec, kt_spec, v_spec,
                  pl.BlockSpec((1, 2, t, t), lambda b, h, i: (h, 0, 0, 0))],
        out_specs=q_spec,
        out_shape=jax.ShapeDtypeStruct((batch, seq, d), BF16),
        scratch_shapes=[pltpu.VMEM((2, t, LANES), BF16),
                        pltpu.VMEM((2, 2, t, t), F32),
                        pltpu.VMEM((2, 2, t, LANES), F32),
                        pltpu.VMEM((2, t, LANES), F32),
                        pltpu.VMEM((2, t, LANES), F32),
                        pltpu.VMEM((2, t, LANES), F32)],
        compiler_params=_params("parallel", "parallel", "arbitrary"),
        name="df_prompt",
    )(lam_vec, subln_g.reshape(1, LANES), q3, kt, v3, tab)
    return out.reshape(batch * seq, d)


def _head_rows(q_row, col_of_row):
    d = q_row.shape[1]
    q = jnp.broadcast_to(q_row.astype(F32), (DEC_HEADS, d))
    row = lax.broadcasted_iota(jnp.int32, (DEC_HEADS, d), 0)
    col = lax.broadcasted_iota(jnp.int32, (DEC_HEADS, d), 1)
    start = col_of_row(row)
    return jnp.where((col >= start) & (col < start + HEAD_DIM), q, 0.0).astype(BF16)


def _sb_decode_kernel(pt_ref, q_ref, *refs):
    del pt_ref
    npg = (len(refs) - 3) // 2
    kt_refs, vt_refs = refs[:npg], refs[npg:2 * npg]
    o_ref, acc_ref, c_ref = refs[2 * npg:]
    s = pl.program_id(1)

    @pl.when(s == 0)
    def _():
        acc_ref[...] = jnp.zeros_like(acc_ref)
        c_ref[...] = jnp.zeros_like(c_ref)

    qrows = _head_rows(q_ref[0], lambda r: r * HEAD_DIM)
    u = _suffix_matrix(PAGE_SIZE)
    cc = c_ref[...]
    for p in range(npg):
        z = _dot(qrows, kt_refs[p][0].astype(BF16))
        nls = _softplus2(z)
        after = _dot(nls.astype(BF16), u)
        a = jnp.exp2(z - nls + after + cc)
        for h in range(DEC_HEADS):
            rows = slice(h * HEAD_DIM, (h + 1) * HEAD_DIM)
            acc_ref[rows, :] += vt_refs[p][0, rows, :] * a[h:h + 1, :]
        cc = cc + after[:, 0:1] - _bf16_round(nls[:, 0:1])
    c_ref[...] = cc

    @pl.when(s == pl.num_programs(1) - 1)
    def _():
        o_ref[0] = jnp.sum(acc_ref[...].T, axis=0, keepdims=True).astype(o_ref.dtype)


def _page_specs(n_pages, npg, block, reverse, first_page):
    specs = []
    for p in range(npg):
        if reverse:
            idx = lambda b, s, pt, p=p: (first_page + pt[b, n_pages - 1 - (s * npg + p)], 0, 0)
        else:
            idx = lambda b, s, pt, p=p: (first_page + pt[b, s * npg + p], 0, 0)
        specs.append(pl.BlockSpec((1,) + block, idx))
    return specs


def _sb_decode_attention(q, cache_kt, cache_vt, page_table, first_page):
    db, d = q.shape
    n_pages = page_table.shape[1]
    npg = min(DEC_PAGES_PER_STEP, n_pages)
    row = pl.BlockSpec((1, 1, d), lambda b, s, pt: (b, 0, 0))
    pages = lambda: _page_specs(n_pages, npg, (d, PAGE_SIZE), True, first_page)
    out = pl.pallas_call(
        _sb_decode_kernel,
        grid_spec=pltpu.PrefetchScalarGridSpec(
            num_scalar_prefetch=1,
            grid=(db, n_pages // npg),
            in_specs=[row] + pages() + pages(),
            out_specs=row,
            scratch_shapes=[pltpu.VMEM((d, PAGE_SIZE), F32), pltpu.VMEM((DEC_HEADS, 1), F32)]),
        out_shape=jax.ShapeDtypeStruct((db, 1, d), BF16),
        compiler_params=_params("parallel", "arbitrary"),
        name="sb_decode",
    )(page_table, q.reshape(db, 1, d), *([cache_kt] * npg), *([cache_vt] * npg))
    return out.reshape(db, d)


def _df_decode_kernel(pt_ref, lv_ref, g_ref, dec_ref, new_ref, spread_ref, q_ref, kn_ref, vn_ref, *refs, lam_init):
    del pt_ref
    npg = (len(refs) - 4) // 2
    kt_refs, v_refs = refs[:npg], refs[npg:2 * npg]
    o_ref, acc_ref, m_ref, l_ref = refs[2 * npg:]
    s = pl.program_id(1)
    last = s == pl.num_programs(1) - 1
    n_heads = DEC_HEADS // 2

    @pl.when(s == 0)
    def _():
        acc_ref[...] = jnp.zeros_like(acc_ref)
        m_ref[...] = jnp.full_like(m_ref, NEG_BIG)
        l_ref[...] = jnp.zeros_like(l_ref)

    qrows = _head_rows(q_ref[0], lambda r: (r % n_heads) * LANES + (r // n_heads) * HEAD_DIM)
    scores = []
    for p in range(npg):
        sp = _dot(qrows, kt_refs[p][0].astype(BF16))
        if p == npg - 1:
            sp = sp + jnp.where(last, dec_ref[...], 0.0)
        scores.append(sp)
    m_old = m_ref[...]
    m_new = m_old
    for sp in scores:
        m_new = jnp.maximum(m_new, jnp.max(sp, axis=1, keepdims=True))
    alpha = jnp.exp2(m_old - m_new)
    l = alpha * l_ref[...]
    acc = alpha * acc_ref[...]
    row = lax.broadcasted_iota(jnp.int32, (DEC_HEADS, PAGE_SIZE * n_heads), 0)
    lane = lax.broadcasted_iota(jnp.int32, (DEC_HEADS, PAGE_SIZE * n_heads), 1)
    own_head = lane % n_heads == row % n_heads
    for p, sp in enumerate(scores):
        pr = jnp.exp2(sp - m_new)
        l = l + jnp.sum(pr, axis=1, keepdims=True)
        spread = _dot(pr.astype(BF16), spread_ref[...])
        spread = jnp.where(own_head, spread, 0.0).astype(BF16)
        acc = acc + _dot(spread, v_refs[p][0].astype(BF16))
    m_ref[...] = m_new
    l_ref[...] = l
    acc_ref[...] = acc

    @pl.when(last)
    def _():
        kn = kn_ref[0].astype(F32)
        s_new = jnp.sum(qrows.astype(F32) * kn, axis=1, keepdims=True) + new_ref[...][:, 0:1]
        m_fin = jnp.maximum(m_new, s_new)
        a_old = jnp.exp2(m_new - m_fin)
        p_new = jnp.exp2(s_new - m_fin)
        l_fin = a_old * l + p_new
        vn = vn_ref[0]
        acc_fin = a_old * acc + p_new * jnp.concatenate([vn, vn], axis=0)
        norm = acc_fin / l_fin
        lam = _df_lambda(lv_ref, lam_init)
        o = norm[0:n_heads] - lam * norm[n_heads:DEC_HEADS]
        o = o * lax.rsqrt(jnp.mean(o * o, axis=1, keepdims=True) + 1e-5)
        o_ref[0] = (o * g_ref[...] * (1.0 - lam_init)).astype(o_ref.dtype)


def _df_decode_attention(q, k_new, v_new, cache_kt, cache_v, page_table, first_page, dec_bias, new_bias,
                         lam_vec, subln_g, lam_init):
    db, d = q.shape
    n_pages = page_table.shape[1]
    npg = min(DEC_PAGES_PER_STEP, n_pages)
    n_heads = d // LANES
    row = pl.BlockSpec((1, 1, d), lambda b, s, pt: (b, 0, 0))
    head_rows = pl.BlockSpec((1, n_heads, LANES), lambda b, s, pt: (b, 0, 0))
    const = lambda shape: pl.BlockSpec(shape, lambda b, s, pt: (0,) * len(shape))
    kt_pages = _page_specs(n_pages, npg, (d, PAGE_SIZE), False, first_page)
    v_pages = _page_specs(n_pages, npg, (PAGE_SIZE * n_heads, LANES), False, first_page)
    spread = np.repeat(np.eye(PAGE_SIZE), n_heads, axis=1)
    out = pl.pallas_call(
        functools.partial(_df_decode_kernel, lam_init=lam_init),
        grid_spec=pltpu.PrefetchScalarGridSpec(
            num_scalar_prefetch=1,
            grid=(db, n_pages // npg),
            in_specs=[const(lam_vec.shape), const((1, LANES)), const((DEC_HEADS, PAGE_SIZE)),
                      const((DEC_HEADS, LANES)), const(spread.shape), row, row, head_rows] + kt_pages + v_pages,
            out_specs=head_rows,
            scratch_shapes=[pltpu.VMEM((DEC_HEADS, LANES), F32), pltpu.VMEM((DEC_HEADS, 1), F32),
                            pltpu.VMEM((DEC_HEADS, 1), F32)]),
        out_shape=jax.ShapeDtypeStruct((db, n_heads, LANES), BF16),
        compiler_params=_params("parallel", "arbitrary"),
        name="df_decode",
    )(page_table, lam_vec, subln_g.reshape(1, LANES), dec_bias, new_bias, jnp.asarray(spread, BF16),
      q.reshape(db, 1, d), k_new.reshape(db, 1, d), v_new.reshape(db, n_heads, LANES),
      *([cache_kt] * npg), *([cache_v] * npg))
    return out.reshape(db, d)


def kernel(x_prompt, x_sample, cache_sb_k, cache_sb_v, cache_df_k, cache_df_v, page_table,
           w_in, w_out, ln_g, ln_b, df_lambda, df_subln_g, rel_bias, w_up, w_down):
    batch, seq, d = x_prompt.shape
    db = x_sample.shape[0]
    depth = w_in.shape[0]
    n_sb_heads = d // HEAD_DIM
    n_df_heads = d // (2 * HEAD_DIM)
    alpha = (2 * depth) ** 0.25
    q_scale = ATT_SCALE * LOG2E
    t = min(ATT_BLOCK, seq)
    assert x_sample.shape[1] == 1 and d // HEAD_DIM == DEC_HEADS
    assert cache_sb_k.shape[2] == PAGE_SIZE and seq % t == 0 and t % MXU_DIM == 0

    xp = x_prompt.reshape(batch * seq, d)
    xs = x_sample.reshape(db, d)
    w_in_b, w_out_b = w_in.astype(BF16), w_out.astype(BF16)
    w_up_b, w_down_b = w_up.astype(BF16), w_down.astype(BF16)
    n_pool = cache_sb_k.shape[1]
    feature_major = lambda cache: jnp.moveaxis(cache, 2, -1).reshape(-1, d, PAGE_SIZE)
    df_v_pages = cache_df_v.reshape(-1, PAGE_SIZE * n_df_heads, 2 * HEAD_DIM)
    token_major = lambda xt, feat: jnp.moveaxis(xt.reshape((batch,) + feat + (seq,)), -1, 1)

    sb_kp, sb_vp, sb_ks, sb_vs = [], [], [], []
    df_kp, df_vp, df_ks, df_vs = [], [], [], []
    for i in range(depth):
        j = i // 2
        qp, kpt, kptb, vp, vpb = _qkv_proj_fm(xp, w_in_b[i], q_scale, batch, v_feature_major=(i % 2 == 0))
        qs, ks_, vs_ = _qkv_proj(xs, w_in_b[i], q_scale)
        if i % 2 == 0:
            op = _sb_prompt_attention(qp, kptb, vpb, batch, seq)
            os_ = _sb_decode_attention(qs, feature_major(cache_sb_k), feature_major(cache_sb_v),
                                       page_table, j * n_pool)
            sb_kp.append(token_major(kpt, (n_sb_heads, HEAD_DIM)))
            sb_vp.append(token_major(vp, (n_sb_heads, HEAD_DIM)))
            sb_ks.append(ks_.reshape(db, 1, n_sb_heads, HEAD_DIM))
            sb_vs.append(vs_.reshape(db, 1, n_sb_heads, HEAD_DIM))
        else:
            lam_init = 0.8 - 0.6 * math.exp(-0.3 * i)
            tab, dec_bias, new_bias = _bias_tables(rel_bias, t)
            dec_bias = jnp.tile(dec_bias.reshape(n_df_heads, PAGE_SIZE), (2, 1))
            new_bias = jnp.tile(new_bias.reshape(n_df_heads, LANES), (2, 1))
            op = _df_prompt_attention(qp, kptb, vpb, tab, df_lambda[j], df_subln_g[j], batch, seq, lam_init)
            os_ = _df_decode_attention(qs, ks_, vs_, feature_major(cache_df_k), df_v_pages, page_table, j * n_pool,
                                       dec_bias, new_bias, df_lambda[j], df_subln_g[j], lam_init)
            df_kp.append(token_major(kpt, (n_df_heads, 2, HEAD_DIM)))
            df_vp.append(vp.reshape(batch, seq, n_df_heads, 2 * HEAD_DIM))
            df_ks.append(ks_.reshape(db, 1, n_df_heads, 2, HEAD_DIM))
            df_vs.append(vs_.reshape(db, 1, n_df_heads, 2 * HEAD_DIM))
        xp = _outproj_ln(op, xp, w_out_b[i], ln_g[i, 0], ln_b[i, 0], alpha)
        xs = _outproj_ln(os_, xs, w_out_b[i], ln_g[i, 0], ln_b[i, 0], alpha)
        xp = _mlp_ln(xp, w_up_b[i], w_down_b[i], ln_g[i, 1], ln_b[i, 1], alpha)
        xs = _mlp_ln(xs, w_up_b[i], w_down_b[i], ln_g[i, 1], ln_b[i, 1], alpha)
    return (xp.reshape(batch, seq, d), xs.reshape(db, 1, d),
            jnp.stack(sb_kp), jnp.stack(sb_vp), jnp.stack(sb_ks), jnp.stack(sb_vs),
            jnp.stack(df_kp), jnp.stack(df_vp), jnp.stack(df_ks), jnp.stack(df_vs))
```

```python
import functools
import math

import numpy as np
import jax
import jax.numpy as jnp
from jax import lax
from jax.experimental import pallas as pl
from jax.experimental.pallas import tpu as pltpu

HEAD_DIM = 64
PAGE_SIZE = 128
REL_BUCKETS = 32
REL_MAX_DIST = 128
ATT_SCALE = HEAD_DIM ** -0.5
NEG_BIG = -1e30
LN_EPS = 1e-5
LOG2E = math.log2(math.e)
F32_EXP2_ZERO = -150.0

LANES = 128
MXU_DIM = 256
VMEM_LIMIT_BYTES = 48 * 1024 * 1024

ATT_BLOCK = 2 * MXU_DIM
ROW_BLOCK = 512
FF_CHUNK = 1024
DEC_HEADS = 16
DEC_PAGES_PER_STEP = 16

F32 = jnp.float32
BF16 = jnp.bfloat16


def _params(*semantics):
    return pltpu.CompilerParams(dimension_semantics=semantics,
                                vmem_limit_bytes=VMEM_LIMIT_BYTES)


def _dot(a, b):
    return jnp.dot(a, b, preferred_element_type=F32)


def _dot_nt(a, b):
    return lax.dot_general(a, b, (((1,), (1,)), ((), ())), preferred_element_type=F32)


def _suffix_matrix_np(n):
    return np.where(np.arange(n)[:, None] > np.arange(n)[None, :], -1.0, 0.0)


def _suffix_matrix(n):
    r = lax.broadcasted_iota(jnp.int32, (n, n), 0)
    c = lax.broadcasted_iota(jnp.int32, (n, n), 1)
    return jnp.where(r > c, -1.0, 0.0).astype(BF16)


def _split_halves(q2):
    qf = q2.astype(F32)
    first = lax.broadcasted_iota(jnp.int32, qf.shape, 1) < HEAD_DIM
    return jnp.where(first, qf, 0.0).astype(BF16), jnp.where(first, 0.0, qf).astype(BF16)


def _bf16_round(x):
    return x.astype(BF16).astype(F32)


def _rep(col):
    return jnp.broadcast_to(col, (col.shape[0], LANES))


def _lane_blocks(x):
    return [x[:, cb * LANES:(cb + 1) * LANES] for cb in range(x.shape[1] // LANES)]


def _softplus2(z):
    neg_abs = pltpu.bitcast(pltpu.bitcast(z, jnp.uint32) | jnp.uint32(0x80000000), F32)
    return jnp.maximum(z, 0.0) + jnp.log2(1.0 + jnp.exp2(neg_abs))


def _proj_kernel(x_ref, w_ref, q_ref, k_ref, v_ref, *, q_scale):
    d = x_ref.shape[1]
    x = x_ref[...].astype(BF16)
    q_ref[...] = (_dot(x, w_ref[:, 0:d]) * q_scale).astype(BF16)
    k_ref[...] = _dot(x, w_ref[:, d:2 * d])
    v_ref[...] = _dot(x, w_ref[:, 2 * d:3 * d])


def _qkv_proj(x, w_bf16, q_scale):
    m, d = x.shape
    bm = min(ROW_BLOCK, m)
    row = pl.BlockSpec((bm, d), lambda i: (i, 0))
    return pl.pallas_call(
        functools.partial(_proj_kernel, q_scale=q_scale),
        grid=(m // bm,),
        in_specs=[row, pl.BlockSpec((d, 3 * d), lambda i: (0, 0))],
        out_specs=[row, row, row],
        out_shape=[jax.ShapeDtypeStruct((m, d), BF16),
                   jax.ShapeDtypeStruct((m, d), F32),
                   jax.ShapeDtypeStruct((m, d), F32)],
        compiler_params=_params("parallel"),
        name="qkv_proj",
    )(x, w_bf16)


def _proj_fm_kernel(x_ref, w_ref, q_ref, kt_ref, ktb_ref, v_ref, vb_ref, *, q_scale, v_feature_major):
    d = x_ref.shape[1]
    x = x_ref[...].astype(BF16)
    q_ref[...] = (_dot(x, w_ref[:, 0:d]) * q_scale).astype(BF16)
    kt = _dot(x, w_ref[:, d:2 * d]).T
    kt_ref[0] = kt
    ktb_ref[0] = kt.astype(BF16)
    v = _dot(x, w_ref[:, 2 * d:3 * d])
    vb_ref[...] = v.astype(BF16)
    if v_feature_major:
        v_ref[0] = v.T
    else:
        v_ref[...] = v


def _qkv_proj_fm(x, w_bf16, q_scale, batch, v_feature_major):
    m, d = x.shape
    seq = m // batch
    bm = min(ROW_BLOCK, seq)
    nb = seq // bm
    row = pl.BlockSpec((bm, d), lambda i: (i, 0))
    fm = pl.BlockSpec((1, d, bm), lambda i: (i // nb, 0, i % nb))
    fm_f32 = jax.ShapeDtypeStruct((batch, d, seq), F32)
    return pl.pallas_call(
        functools.partial(_proj_fm_kernel, q_scale=q_scale, v_feature_major=v_feature_major),
        grid=(m // bm,),
        in_specs=[row, pl.BlockSpec((d, 3 * d), lambda i: (0, 0))],
        out_specs=[row, fm, fm, fm if v_feature_major else row, row],
        out_shape=[jax.ShapeDtypeStruct((m, d), BF16),
                   fm_f32,
                   jax.ShapeDtypeStruct((batch, d, seq), BF16),
                   fm_f32 if v_feature_major else jax.ShapeDtypeStruct((m, d), F32),
                   jax.ShapeDtypeStruct((m, d), BF16)],
        compiler_params=_params("parallel"),
        name="qkv_proj_fm",
    )(x, w_bf16)


def _layernorm(xf, g, b):
    mu = jnp.mean(xf, axis=-1, keepdims=True)
    xc = xf - mu
    var = jnp.mean(xc * xc, axis=-1, keepdims=True)
    return xc * lax.rsqrt(var + LN_EPS) * g + b


def _outproj_ln_kernel(o_ref, x_ref, w_ref, g_ref, b_ref, y_ref, *, alpha):
    m = _dot(o_ref[...], w_ref[...])
    y_ref[...] = _layernorm(alpha * x_ref[...] + m, g_ref[...], b_ref[...])


def _outproj_ln(o_bf16, x, w_bf16, g, b, alpha):
    m, d = x.shape
    bm = min(ROW_BLOCK, m)
    row = pl.BlockSpec((bm, d), lambda i: (i, 0))
    vec = pl.BlockSpec((1, d), lambda i: (0, 0))
    return pl.pallas_call(
        functools.partial(_outproj_ln_kernel, alpha=alpha),
        grid=(m // bm,),
        in_specs=[row, row, pl.BlockSpec((d, d), lambda i: (0, 0)), vec, vec],
        out_specs=row,
        out_shape=jax.ShapeDtypeStruct((m, d), F32),
        compiler_params=_params("parallel"),
        name="outproj_ln",
    )(o_bf16, x, w_bf16, g.reshape(1, d), b.reshape(1, d))


def _mlp_ln_kernel(x_ref, wu_ref, wd_ref, g_ref, b_ref, y_ref, acc_ref, *, alpha):
    c = pl.program_id(1)
    x = x_ref[...]
    h = jnp.maximum(_dot(x.astype(BF16), wu_ref[...]), 0.0)
    part = _dot((h * h).astype(BF16), wd_ref[...])

    @pl.when(c == 0)
    def _():
        acc_ref[...] = part

    @pl.when(c > 0)
    def _():
        acc_ref[...] += part

    @pl.when(c == pl.num_programs(1) - 1)
    def _():
        y_ref[...] = _layernorm(alpha * x + acc_ref[...], g_ref[...], b_ref[...])


def _mlp_ln(x, wu_bf16, wd_bf16, g, b, alpha):
    m, d = x.shape
    dff = wu_bf16.shape[1]
    bm = min(ROW_BLOCK, m)
    fc = min(FF_CHUNK, dff)
    row = pl.BlockSpec((bm, d), lambda i, c: (i, 0))
    vec = pl.BlockSpec((1, d), lambda i, c: (0, 0))
    return pl.pallas_call(
        functools.partial(_mlp_ln_kernel, alpha=alpha),
        grid=(m // bm, dff // fc),
        in_specs=[row,
                  pl.BlockSpec((d, fc), lambda i, c: (0, c)),
                  pl.BlockSpec((fc, d), lambda i, c: (c, 0)),
                  vec, vec],
        out_specs=row,
        out_shape=jax.ShapeDtypeStruct((m, d), F32),
        scratch_shapes=[pltpu.VMEM((bm, d), F32)],
        compiler_params=_params("parallel", "arbitrary"),
        name="mlp_ln",
    )(x, wu_bf16, wd_bf16, g.reshape(1, d), b.reshape(1, d))


def _sb_scores(qh, kt, mask, nls_ref, nls0_ref, logb_ref):
    z = _dot(qh, kt)
    nls = _softplus2(z)
    logb = z - nls
    if mask is not None:
        nls = jnp.where(mask, nls, 0.0)
        logb = jnp.where(mask, logb, NEG_BIG)
    nls_ref[...] = nls.astype(BF16)
    for ch in range(nls.shape[1] // MXU_DIM):
        nls0_ref[ch] = _rep(_bf16_round(nls[:, ch * MXU_DIM:ch * MXU_DIM + 1]))
    logb_ref[...] = logb


def _sb_weights(nls_ref, nls0_ref, logb_ref, vb, u, acc_ref, c_ref):
    t = nls_ref.shape[0]
    n_chunks = nls_ref.shape[1] // MXU_DIM
    later = jnp.zeros((t, LANES), F32)
    a = [None] * n_chunks
    for ch in reversed(range(n_chunks)):
        cols = slice(ch * MXU_DIM, (ch + 1) * MXU_DIM)
        after = _dot(nls_ref[:, cols], u)
        blocks = []
        for lb in range(MXU_DIM // LANES):
            lanes = slice(lb * LANES, (lb + 1) * LANES)
            e = logb_ref[:, cols][:, lanes] + after[:, lanes]
            blocks.append(jnp.exp2(e if ch == n_chunks - 1 else e + later).astype(BF16))
        a[ch] = jnp.concatenate(blocks, axis=1)
        later = later + (_rep(after[:, 0:1]) - nls0_ref[ch])
    o = _dot(jnp.concatenate(a, axis=1), vb)
    c = c_ref[...]
    acc_ref[...] += jnp.exp2(c) * o
    c_ref[...] = c + later


def _sb_prompt_kernel(u_ref, q_ref, kt_ref, v_ref, o_ref, qs_ref, nls_ref, nls0_ref, logb_ref, acc_ref, c_ref):
    t = q_ref.shape[1]
    i = pl.program_id(2)
    qs_ref[0], qs_ref[1] = _split_halves(q_ref[0])
    acc_ref[...] = jnp.zeros_like(acc_ref)
    c_ref[...] = jnp.zeros_like(c_ref)

    def scores(j, slot, mask=None):
        kt = kt_ref[0, :, pl.ds(pl.multiple_of(j * t, t), t)]
        for h in range(2):
            _sb_scores(qs_ref[h], kt, mask, nls_ref.at[slot, h], nls0_ref.at[slot, h], logb_ref.at[slot, h])

    def weights(j, slot):
        vb = v_ref[0, pl.ds(pl.multiple_of(j * t, t), t), :]
        for h in range(2):
            _sb_weights(nls_ref.at[slot, h], nls0_ref.at[slot, h], logb_ref.at[slot, h], vb, u_ref[...],
                        acc_ref.at[h], c_ref.at[h])

    r = lax.broadcasted_iota(jnp.int32, (t, t), 0)
    c = lax.broadcasted_iota(jnp.int32, (t, t), 1)
    scores(i, 0, c < r)

    def stick_left():
        return jnp.max(c_ref[...]) > F32_EXP2_ZERO

    def body(state):
        s, _ = state
        j = i - 1 - 2 * s
        scores(j, 1)
        weights(j + 1, 0)
        scores(j - 1, 0)
        weights(j, 1)
        return s + 1, stick_left()

    _, alive = lax.while_loop(lambda state: (state[0] < i // 2) & state[1], body, (jnp.int32(0), True))

    @pl.when(alive & (i % 2 == 1))
    def _():
        scores(0, 1)
        weights(1, 0)
        weights(0, 1)

    @pl.when(alive & (i % 2 == 0))
    def _():
        weights(0, 0)

    lane_o = lax.broadcasted_iota(jnp.int32, (t, LANES), 1)
    o_ref[0] = jnp.where(lane_o < HEAD_DIM, acc_ref[0], acc_ref[1]).astype(o_ref.dtype)


def _sb_prompt_attention(q, kt, v, batch, seq):
    d = q.shape[1]
    t = min(ATT_BLOCK, seq)
    q3, v3 = (a.reshape(batch, seq, d) for a in (q, v))
    kt_spec = pl.BlockSpec((1, LANES, seq), lambda b, h, i: (b, h, 0))
    v_spec = pl.BlockSpec((1, seq, LANES), lambda b, h, i: (b, 0, h))
    q_spec = pl.BlockSpec((1, t, LANES), lambda b, h, i: (b, i, h))
    out = pl.pallas_call(
        _sb_prompt_kernel,
        grid=(batch, d // LANES, seq // t),
        in_specs=[pl.BlockSpec((MXU_DIM, MXU_DIM), lambda b, h, i: (0, 0)), q_spec, kt_spec, v_spec],
        out_specs=q_spec,
        out_shape=jax.ShapeDtypeStruct((batch, seq, d), BF16),
        scratch_shapes=[pltpu.VMEM((2, t, LANES), BF16),
                        pltpu.VMEM((2, 2, t, t), BF16),
                        pltpu.VMEM((2, 2, t // MXU_DIM, t, LANES), F32),
                        pltpu.VMEM((2, 2, t, t), F32),
                        pltpu.VMEM((2, t, LANES), F32),
                        pltpu.VMEM((2, t, LANES), F32)],
        compiler_params=_params("parallel", "parallel", "arbitrary"),
        name="sb_prompt",
    )(jnp.asarray(_suffix_matrix_np(MXU_DIM), BF16), q3, kt, v3)
    return out.reshape(batch * seq, d)


def _t5_bucket_np(dist):
    n = np.maximum(dist, 0).astype(np.int32)
    max_exact = REL_BUCKETS // 2
    nf = np.maximum(n, 1).astype(np.float32)
    large = max_exact + (np.log(nf / np.float32(max_exact)) / np.float32(math.log(REL_MAX_DIST / max_exact))
                         * np.float32(REL_BUCKETS - max_exact)).astype(np.int32)
    large = np.minimum(large, REL_BUCKETS - 1)
    return np.where(n < max_exact, n, large).astype(np.int32)


def _bias_table_kernel(rel_ref, bsub_ref, bdiag_ref, bdec_ref, tab_ref, dec_ref, new_ref):
    h = pl.program_id(0)
    far = rel_ref[REL_BUCKETS - 1, h]

    def gather(bucket):
        out = jnp.zeros(bucket.shape, F32)
        for b in range(REL_BUCKETS):
            out = jnp.where(bucket == b, rel_ref[b, h], out)
        return (out - far) * LOG2E

    tab_ref[0, 0] = gather(bsub_ref[...])
    tab_ref[0, 1] = gather(bdiag_ref[...])
    dec_ref[0] = gather(bdec_ref[...])
    new_ref[0] = jnp.zeros(new_ref.shape[1:], F32) + (rel_ref[0, h] - far) * LOG2E


def _bias_tables(rel_bias, t):
    assert t >= REL_MAX_DIST and PAGE_SIZE >= REL_MAX_DIST
    n_heads = rel_bias.shape[1]
    r = np.arange(t)[:, None]
    c = np.arange(t)[None, :]
    b_sub = _t5_bucket_np(r - c + t)
    b_diag = _t5_bucket_np(r - c)
    b_dec = _t5_bucket_np(PAGE_SIZE - np.arange(PAGE_SIZE))[None, :]
    full = lambda shape: pl.BlockSpec(shape, lambda h: (0,) * len(shape))
    return pl.pallas_call(
        _bias_table_kernel,
        grid=(n_heads,),
        in_specs=[pl.BlockSpec(memory_space=pltpu.SMEM), full((t, t)), full((t, t)), full((1, PAGE_SIZE))],
        out_specs=[pl.BlockSpec((1, 2, t, t), lambda h: (h, 0, 0, 0)),
                   pl.BlockSpec((1, 1, PAGE_SIZE), lambda h: (h, 0, 0)),
                   pl.BlockSpec((1, 1, LANES), lambda h: (h, 0, 0))],
        out_shape=[jax.ShapeDtypeStruct((n_heads, 2, t, t), F32),
                   jax.ShapeDtypeStruct((n_heads, 1, PAGE_SIZE), F32),
                   jax.ShapeDtypeStruct((n_heads, 1, LANES), F32)],
        compiler_params=_params("arbitrary"),
        name="t5_bias_tables",
    )(rel_bias, jnp.asarray(b_sub), jnp.asarray(b_diag), jnp.asarray(b_dec))


def _df_lambda(lv_ref, lam_init):
    lv = lv_ref[...]
    a = jnp.sum(lv[0:1, :] * lv[1:2, :], axis=1, keepdims=True)
    b = jnp.sum(lv[2:3, :] * lv[3:4, :], axis=1, keepdims=True)
    return jnp.exp(a) - jnp.exp(b) + lam_init


def _df_prompt_kernel(lv_ref, g_ref, q_ref, kt_ref, v_ref, tab_ref, o_ref,
                      qs_ref, s_ref, rmax_ref, m_ref, l_ref, acc_ref, *, lam_init):
    t = q_ref.shape[1]
    i = pl.program_id(2)
    qs_ref[0], qs_ref[1] = _split_halves(q_ref[0])
    m_ref[...] = jnp.full_like(m_ref, NEG_BIG)
    l_ref[...] = jnp.zeros_like(l_ref)
    acc_ref[...] = jnp.zeros_like(acc_ref)

    def scores(j, slot, bias=None, mask=None):
        kt = kt_ref[0, :, pl.ds(pl.multiple_of(j * t, t), t)]
        for c in range(2):
            s = _dot(qs_ref[c], kt)
            if bias is not None:
                s = s + bias
            if mask is not None:
                s = jnp.where(mask, s, NEG_BIG)
            s_ref[slot, c] = s
            rmax_ref[slot, c] = _rep(jnp.max(s, axis=1, keepdims=True))

    def update(j, slot):
        vb = v_ref[0, pl.ds(pl.multiple_of(j * t, t), t), :]
        for c in range(2):
            m = m_ref[c]
            m_new = jnp.maximum(m, rmax_ref[slot, c])
            alpha = jnp.exp2(m - m_new)
            p = [jnp.exp2(sb - m_new) for sb in _lane_blocks(s_ref[slot, c])]
            psum = p[0]
            for pb in p[1:]:
                psum = psum + pb
            l_ref[c] = alpha * l_ref[c] + _rep(jnp.sum(psum, axis=1, keepdims=True))
            pv = _dot(jnp.concatenate([pb.astype(BF16) for pb in p], axis=1), vb)
            acc_ref[c] = alpha * acc_ref[c] + pv
            m_ref[c] = m_new

    r = lax.broadcasted_iota(jnp.int32, (t, t), 0)
    c = lax.broadcasted_iota(jnp.int32, (t, t), 1)
    scores(i, 0, tab_ref[0, 1], c <= r)

    @pl.when(i > 0)
    def _():
        scores(i - 1, 1, tab_ref[0, 0])
        update(i, 0)

    def body(s, _):
        j = i - 2 - 2 * s
        scores(j, 0)
        update(j + 1, 1)
        scores(j - 1, 1)
        update(j, 0)
        return 0

    n_far = jnp.maximum(i - 1, 0)
    lax.fori_loop(0, n_far // 2, body, 0)

    @pl.when(n_far % 2 == 1)
    def _():
        scores(0, 0)
        update(1, 1)
        update(0, 0)

    @pl.when((n_far % 2 == 0) & (i > 0))
    def _():
        update(0, 1)

    @pl.when(i == 0)
    def _():
        update(0, 0)

    lam = _df_lambda(lv_ref, lam_init)
    o = acc_ref[0] / l_ref[0] - lam * (acc_ref[1] / l_ref[1])
    o = o * lax.rsqrt(jnp.mean(o * o, axis=1, keepdims=True) + 1e-5)
    o_ref[0] = (o * g_ref[...] * (1.0 - lam_init)).astype(o_ref.dtype)


def _df_prompt_attention(q, kt, v, tab, lam_vec, subln_g, batch, seq, lam_init):
    d = q.shape[1]
    t = tab.shape[2]
    q3, v3 = (a.reshape(batch, seq, d) for a in (q, v))
    kt_spec = pl.BlockSpec((1, LANES, seq), lambda b, h, i: (b, h, 0))
    v_spec = pl.BlockSpec((1, seq, LANES), lambda b, h, i: (b, 0, h))
    q_spec = pl.BlockSpec((1, t, LANES), lambda b, h, i: (b, i, h))
    out = pl.pallas_call(
        functools.partial(_df_prompt_kernel, lam_init=lam_init),
        grid=(batch, d // LANES, seq // t),
        in_specs=[pl.BlockSpec(lam_vec.shape, lambda b, h, i: (0, 0)),
                  pl.BlockSpec((1, LANES), lambda b, h, i: (0, 0)),
                  q_spec, kt_spec, v_spec,
                  pl.BlockSpec((1, 2, t, t), lambda b, h, i: (h, 0, 0, 0))],
        out_specs=q_spec,
        out_shape=jax.ShapeDtypeStruct((batch, seq, d), BF16),
        scratch_shapes=[pltpu.VMEM((2, t, LANES), BF16),
                        pltpu.VMEM((2, 2, t, t), F32),
                        pltpu.VMEM((2, 2, t, LANES), F32),
                        pltpu.VMEM((2, t, LANES), F32),
                        pltpu.VMEM((2, t, LANES), F32),
                        pltpu.VMEM((2, t, LANES), F32)],
        compiler_params=_params("parallel", "parallel", "arbitrary"),
        name="df_prompt",
    )(lam_vec, subln_g.reshape(1, LANES), q3, kt, v3, tab)
    return out.reshape(batch * seq, d)


def _head_rows(q_row, col_of_row):
    d = q_row.shape[1]
    q = jnp.broadcast_to(q_row.astype(F32), (DEC_HEADS, d))
    row = lax.broadcasted_iota(jnp.int32, (DEC_HEADS, d), 0)
    col = lax.broadcasted_iota(jnp.int32, (DEC_HEADS, d), 1)
    start = col_of_row(row)
    return jnp.where((col >= start) & (col < start + HEAD_DIM), q, 0.0).astype(BF16)


def _sb_decode_kernel(pt_ref, q_ref, *refs):
    del pt_ref
    npg = (len(refs) - 3) // 2
    kt_refs, vt_refs = refs[:npg], refs[npg:2 * npg]
    o_ref, acc_ref, c_ref = refs[2 * npg:]
    s = pl.program_id(1)

    @pl.when(s == 0)
    def _():
        acc_ref[...] = jnp.zeros_like(acc_ref)
        c_ref[...] = jnp.zeros_like(c_ref)

    qrows = _head_rows(q_ref[0], lambda r: r * HEAD_DIM)
    u = _suffix_matrix(PAGE_SIZE)
    cc = c_ref[...]
    for p in range(npg):
        z = _dot(qrows, kt_refs[p][0].astype(BF16))
        nls = _softplus2(z)
        after = _dot(nls.astype(BF16), u)
        a = jnp.exp2(z - nls + after + cc)
        for h in range(DEC_HEADS):
            rows = slice(h * HEAD_DIM, (h + 1) * HEAD_DIM)
            acc_ref[rows, :] += vt_refs[p][0, rows, :] * a[h:h + 1, :]
        cc = cc + after[:, 0:1] - _bf16_round(nls[:, 0:1])
    c_ref[...] = cc

    @pl.when(s == pl.num_programs(1) - 1)
    def _():
        o_ref[0] = jnp.sum(acc_ref[...].T, axis=0, keepdims=True).astype(o_ref.dtype)


def _page_specs(n_pages, npg, block, reverse, first_page):
    specs = []
    for p in range(npg):
        if reverse:
            idx = lambda b, s, pt, p=p: (first_page + pt[b, n_pages - 1 - (s * npg + p)], 0, 0)
        else:
            idx = lambda b, s, pt, p=p: (first_page + pt[b, s * npg + p], 0, 0)
        specs.append(pl.BlockSpec((1,) + block, idx))
    return specs


def _sb_decode_attention(q, cache_kt, cache_vt, page_table, first_page):
    db, d = q.shape
    n_pages = page_table.shape[1]
    npg = min(DEC_PAGES_PER_STEP, n_pages)
    row = pl.BlockSpec((1, 1, d), lambda b, s, pt: (b, 0, 0))
    pages = lambda: _page_specs(n_pages, npg, (d, PAGE_SIZE), True, first_page)
    out = pl.pallas_call(
        _sb_decode_kernel,
        grid_spec=pltpu.PrefetchScalarGridSpec(
            num_scalar_prefetch=1,
            grid=(db, n_pages // npg),
            in_specs=[row] + pages() + pages(),
            out_specs=row,
            scratch_shapes=[pltpu.VMEM((d, PAGE_SIZE), F32), pltpu.VMEM((DEC_HEADS, 1), F32)]),
        out_shape=jax.ShapeDtypeStruct((db, 1, d), BF16),
        compiler_params=_params("parallel", "arbitrary"),
        name="sb_decode",
    )(page_table, q.reshape(db, 1, d), *([cache_kt] * npg), *([cache_vt] * npg))
    return out.reshape(db, d)


def _df_decode_kernel(pt_ref, lv_ref, g_ref, dec_ref, new_ref, spread_ref, q_ref, kn_ref, vn_ref, *refs, lam_init):
    del pt_ref
    npg = (len(refs) - 4) // 2
    kt_refs, v_refs = refs[:npg], refs[npg:2 * npg]
    o_ref, acc_ref, m_ref, l_ref = refs[2 * npg:]
    s = pl.program_id(1)
    last = s == pl.num_programs(1) - 1
    n_heads = DEC_HEADS // 2

    @pl.when(s == 0)
    def _():
        acc_ref[...] = jnp.zeros_like(acc_ref)
        m_ref[...] = jnp.full_like(m_ref, NEG_BIG)
        l_ref[...] = jnp.zeros_like(l_ref)

    qrows = _head_rows(q_ref[0], lambda r: (r % n_heads) * LANES + (r // n_heads) * HEAD_DIM)
    scores = []
    for p in range(npg):
        sp = _dot(qrows, kt_refs[p][0].astype(BF16))
        if p == npg - 1:
            sp = sp + jnp.where(last, dec_ref[...], 0.0)
        scores.append(sp)
    m_old = m_ref[...]
    m_new = m_old
    for sp in scores:
        m_new = jnp.maximum(m_new, jnp.max(sp, axis=1, keepdims=True))
    alpha = jnp.exp2(m_old - m_new)
    l = alpha * l_ref[...]
    acc = alpha * acc_ref[...]
    row = lax.broadcasted_iota(jnp.int32, (DEC_HEADS, PAGE_SIZE * n_heads), 0)
    lane = lax.broadcasted_iota(jnp.int32, (DEC_HEADS, PAGE_SIZE * n_heads), 1)
    own_head = lane % n_heads == row % n_heads
    for p, sp in enumerate(scores):
        pr = jnp.exp2(sp - m_new)
        l = l + jnp.sum(pr, axis=1, keepdims=True)
        spread = _dot(pr.astype(BF16), spread_ref[...])
        spread = jnp.where(own_head, spread, 0.0).astype(BF16)
        acc = acc + _dot(spread, v_refs[p][0].astype(BF16))
    m_ref[...] = m_new
    l_ref[...] = l
    acc_ref[...] = acc

    @pl.when(last)
    def _():
        kn = kn_ref[0].astype(F32)
        s_new = jnp.sum(qrows.astype(F32) * kn, axis=1, keepdims=True) + new_ref[...][:, 0:1]
        m_fin = jnp.maximum(m_new, s_new)
        a_old = jnp.exp2(m_new - m_fin)
        p_new = jnp.exp2(s_new - m_fin)
        l_fin = a_old * l + p_new
        vn = vn_ref[0]
        acc_fin = a_old * acc + p_new * jnp.concatenate([vn, vn], axis=0)
        norm = acc_fin / l_fin
        lam = _df_lambda(lv_ref, lam_init)
        o = norm[0:n_heads] - lam * norm[n_heads:DEC_HEADS]
        o = o * lax.rsqrt(jnp.mean(o * o, axis=1, keepdims=True) + 1e-5)
        o_ref[0] = (o * g_ref[...] * (1.0 - lam_init)).astype(o_ref.dtype)


def _df_decode_attention(q, k_new, v_new, cache_kt, cache_v, page_table, first_page, dec_bias, new_bias,
                         lam_vec, subln_g, lam_init):
    db, d = q.shape
    n_pages = page_table.shape[1]
    npg = min(DEC_PAGES_PER_STEP, n_pages)
    n_heads = d // LANES
    row = pl.BlockSpec((1, 1, d), lambda b, s, pt: (b, 0, 0))
    head_rows = pl.BlockSpec((1, n_heads, LANES), lambda b, s, pt: (b, 0, 0))
    const = lambda shape: pl.BlockSpec(shape, lambda b, s, pt: (0,) * len(shape))
    kt_pages = _page_specs(n_pages, npg, (d, PAGE_SIZE), False, first_page)
    v_pages = _page_specs(n_pages, npg, (PAGE_SIZE * n_heads, LANES), False, first_page)
    spread = np.repeat(np.eye(PAGE_SIZE), n_heads, axis=1)
    out = pl.pallas_call(
        functools.partial(_df_decode_kernel, lam_init=lam_init),
        grid_spec=pltpu.PrefetchScalarGridSpec(
            num_scalar_prefetch=1,
            grid=(db, n_pages // npg),
            in_specs=[const(lam_vec.shape), const((1, LANES)), const((DEC_HEADS, PAGE_SIZE)),
                      const((DEC_HEADS, LANES)), const(spread.shape), row, row, head_rows] + kt_pages + v_pages,
            out_specs=head_rows,
            scratch_shapes=[pltpu.VMEM((DEC_HEADS, LANES), F32), pltpu.VMEM((DEC_HEADS, 1), F32),
                            pltpu.VMEM((DEC_HEADS, 1), F32)]),
        out_shape=jax.ShapeDtypeStruct((db, n_heads, LANES), BF16),
        compiler_params=_params("parallel", "arbitrary"),
        name="df_decode",
    )(page_table, lam_vec, subln_g.reshape(1, LANES), dec_bias, new_bias, jnp.asarray(spread, BF16),
      q.reshape(db, 1, d), k_new.reshape(db, 1, d), v_new.reshape(db, n_heads, LANES),
      *([cache_kt] * npg), *([cache_v] * npg))
    return out.reshape(db, d)


def kernel(x_prompt, x_sample, cache_sb_k, cache_sb_v, cache_df_k, cache_df_v, page_table,
           w_in, w_out, ln_g, ln_b, df_lambda, df_subln_g, rel_bias, w_up, w_down):
    batch, seq, d = x_prompt.shape
    db = x_sample.shape[0]
    depth = w_in.shape[0]
    n_sb_heads = d // HEAD_DIM
    n_df_heads = d // (2 * HEAD_DIM)
    alpha = (2 * depth) ** 0.25
    q_scale = ATT_SCALE * LOG2E
    t = min(ATT_BLOCK, seq)
    assert x_sample.shape[1] == 1 and d // HEAD_DIM == DEC_HEADS
    assert cache_sb_k.shape[2] == PAGE_SIZE and seq % t == 0 and t % MXU_DIM == 0

    xp = x_prompt.reshape(batch * seq, d)
    xs = x_sample.reshape(db, d)
    w_in_b, w_out_b = w_in.astype(BF16), w_out.astype(BF16)
    w_up_b, w_down_b = w_up.astype(BF16), w_down.astype(BF16)
    n_pool = cache_sb_k.shape[1]
    feature_major = lambda cache: jnp.moveaxis(cache, 2, -1).reshape(-1, d, PAGE_SIZE)
    df_v_pages = cache_df_v.reshape(-1, PAGE_SIZE * n_df_heads, 2 * HEAD_DIM)
    token_major = lambda xt, feat: jnp.moveaxis(xt.reshape((batch,) + feat + (seq,)), -1, 1)

    sb_kp, sb_vp, sb_ks, sb_vs = [], [], [], []
    df_kp, df_vp, df_ks, df_vs = [], [], [], []
    for i in range(depth):
        j = i // 2
        qp, kpt, kptb, vp, vpb = _qkv_proj_fm(xp, w_in_b[i], q_scale, batch, v_feature_major=(i % 2 == 0))
        qs, ks_, vs_ = _qkv_proj(xs, w_in_b[i], q_scale)
        if i % 2 == 0:
            op = _sb_prompt_attention(qp, kptb, vpb, batch, seq)
            os_ = _sb_decode_attention(qs, feature_major(cache_sb_k), feature_major(cache_sb_v),
                                       page_table, j * n_pool)
            sb_kp.append(token_major(kpt, (n_sb_heads, HEAD_DIM)))
            sb_vp.append(token_major(vp, (n_sb_heads, HEAD_DIM)))
            sb_ks.append(ks_.reshape(db, 1, n_sb_heads, HEAD_DIM))
            sb_vs.append(vs_.reshape(db, 1, n_sb_heads, HEAD_DIM))
        else:
            lam_init = 0.8 - 0.6 * math.exp(-0.3 * i)
            tab, dec_bias, new_bias = _bias_tables(rel_bias, t)
            dec_bias = jnp.tile(dec_bias.reshape(n_df_heads, PAGE_SIZE), (2, 1))
            new_bias = jnp.tile(new_bias.reshape(n_df_heads, LANES), (2, 1))
            op = _df_prompt_attention(qp, kptb, vpb, tab, df_lambda[j], df_subln_g[j], batch, seq, lam_init)
            os_ = _df_decode_attention(qs, ks_, vs_, feature_major(cache_df_k), df_v_pages, page_table, j * n_pool,
                                       dec_bias, new_bias, df_lambda[j], df_subln_g[j], lam_init)
            df_kp.append(token_major(kpt, (n_df_heads, 2, HEAD_DIM)))
            df_vp.append(vp.reshape(batch, seq, n_df_heads, 2 * HEAD_DIM))
            df_ks.append(ks_.reshape(db, 1, n_df_heads, 2, HEAD_DIM))
            df_vs.append(vs_.reshape(db, 1, n_df_heads, 2 * HEAD_DIM))
        xp = _outproj_ln(op, xp, w_out_b[i], ln_g[i, 0], ln_b[i, 0], alpha)
        xs = _outproj_ln(os_, xs, w_out_b[i], ln_g[i, 0], ln_b[i, 0], alpha)
        xp = _mlp_ln(xp, w_up_b[i], w_down_b[i], ln_g[i, 1], ln_b[i, 1], alpha)
        xs = _mlp_ln(xs, w_up_b[i], w_down_b[i], ln_g[i, 1], ln_b[i, 1], alpha)
    return (xp.reshape(batch, seq, d), xs.reshape(db, 1, d),
            jnp.stack(sb_kp), jnp.stack(sb_vp), jnp.stack(sb_ks), jnp.stack(sb_vs),
            jnp.stack(df_kp), jnp.stack(df_vp), jnp.stack(df_ks), jnp.stack(df_vs))
```

```python
import functools
import math

import numpy as np
import jax
import jax.numpy as jnp
from jax import lax
from jax.experimental import pallas as pl
from jax.experimental.pallas import tpu as pltpu

HEAD_DIM = 64
PAGE_SIZE = 128
REL_BUCKETS = 32
REL_MAX_DIST = 128
ATT_SCALE = HEAD_DIM ** -0.5
NEG_BIG = -1e30
LN_EPS = 1e-5
LOG2E = math.log2(math.e)
F32_EXP2_ZERO = -150.0

LANES = 128
MXU_DIM = 256
VMEM_LIMIT_BYTES = 48 * 1024 * 1024

ATT_BLOCK = 2 * MXU_DIM
ROW_BLOCK = 512
FF_CHUNK = 1024
DEC_HEADS = 16
DEC_PAGES_PER_STEP = 16

F32 = jnp.float32
BF16 = jnp.bfloat16


def _params(*semantics):
    return pltpu.CompilerParams(dimension_semantics=semantics,
                                vmem_limit_bytes=VMEM_LIMIT_BYTES)


def _dot(a, b):
    return jnp.dot(a, b, preferred_element_type=F32)


def _dot_nt(a, b):
    return lax.dot_general(a, b, (((1,), (1,)), ((), ())), preferred_element_type=F32)


def _suffix_matrix_np(n):
    return np.where(np.arange(n)[:, None] > np.arange(n)[None, :], -1.0, 0.0)


def _suffix_matrix(n):
    r = lax.broadcasted_iota(jnp.int32, (n, n), 0)
    c = lax.broadcasted_iota(jnp.int32, (n, n), 1)
    return jnp.where(r > c, -1.0, 0.0).astype(BF16)


def _split_halves(q2):
    qf = q2.astype(F32)
    first = lax.broadcasted_iota(jnp.int32, qf.shape, 1) < HEAD_DIM
    return jnp.where(first, qf, 0.0).astype(BF16), jnp.where(first, 0.0, qf).astype(BF16)


def _bf16_round(x):
    return x.astype(BF16).astype(F32)


def _rep(col):
    return jnp.broadcast_to(col, (col.shape[0], LANES))


def _lane_blocks(x):
    return [x[:, cb * LANES:(cb + 1) * LANES] for cb in range(x.shape[1] // LANES)]


def _softplus2(z):
    neg_abs = pltpu.bitcast(pltpu.bitcast(z, jnp.uint32) | jnp.uint32(0x80000000), F32)
    return jnp.maximum(z, 0.0) + jnp.log2(1.0 + jnp.exp2(neg_abs))


def _proj_kernel(x_ref, w_ref, q_ref, k_ref, v_ref, *, q_scale):
    d = x_ref.shape[1]
    x = x_ref[...].astype(BF16)
    q_ref[...] = (_dot(x, w_ref[:, 0:d]) * q_scale).astype(BF16)
    k_ref[...] = _dot(x, w_ref[:, d:2 * d])
    v_ref[...] = _dot(x, w_ref[:, 2 * d:3 * d])


def _qkv_proj(x, w_bf16, q_scale):
    m, d = x.shape
    bm = min(ROW_BLOCK, m)
    row = pl.BlockSpec((bm, d), lambda i: (i, 0))
    return pl.pallas_call(
        functools.partial(_proj_kernel, q_scale=q_scale),
        grid=(m // bm,),
        in_specs=[row, pl.BlockSpec((d, 3 * d), lambda i: (0, 0))],
        out_specs=[row, row, row],
        out_shape=[jax.ShapeDtypeStruct((m, d), BF16),
                   jax.ShapeDtypeStruct((m, d), F32),
                   jax.ShapeDtypeStruct((m, d), F32)],
        compiler_params=_params("parallel"),
        name="qkv_proj",
    )(x, w_bf16)


def _proj_fm_kernel(x_ref, w_ref, q_ref, kt_ref, ktb_ref, v_ref, vb_ref, *, q_scale, v_feature_major):
    d = x_ref.shape[1]
    x = x_ref[...].astype(BF16)
    q_ref[...] = (_dot(x, w_ref[:, 0:d]) * q_scale).astype(BF16)
    kt = _dot(x, w_ref[:, d:2 * d]).T
    kt_ref[0] = kt
    ktb_ref[0] = kt.astype(BF16)
    v = _dot(x, w_ref[:, 2 * d:3 * d])
    vb_ref[...] = v.astype(BF16)
    if v_feature_major:
        v_ref[0] = v.T
    else:
        v_ref[...] = v


def _qkv_proj_fm(x, w_bf16, q_scale, batch, v_feature_major):
    m, d = x.shape
    seq = m // batch
    bm = min(ROW_BLOCK, seq)
    nb = seq // bm
    row = pl.BlockSpec((bm, d), lambda i: (i, 0))
    fm = pl.BlockSpec((1, d, bm), lambda i: (i // nb, 0, i % nb))
    fm_f32 = jax.ShapeDtypeStruct((batch, d, seq), F32)
    return pl.pallas_call(
        functools.partial(_proj_fm_kernel, q_scale=q_scale, v_feature_major=v_feature_major),
        grid=(m // bm,),
        in_specs=[row, pl.BlockSpec((d, 3 * d), lambda i: (0, 0))],
        out_specs=[row, fm, fm, fm if v_feature_major else row, row],
        out_shape=[jax.ShapeDtypeStruct((m, d), BF16),
                   fm_f32,
                   jax.ShapeDtypeStruct((batch, d, seq), BF16),
                   fm_f32 if v_feature_major else jax.ShapeDtypeStruct((m, d), F32),
                   jax.ShapeDtypeStruct((m, d), BF16)],
        compiler_params=_params("parallel"),
        name="qkv_proj_fm",
    )(x, w_bf16)


def _layernorm(xf, g, b):
    mu = jnp.mean(xf, axis=-1, keepdims=True)
    xc = xf - mu
    var = jnp.mean(xc * xc, axis=-1, keepdims=True)
    return xc * lax.rsqrt(var + LN_EPS) * g + b


def _outproj_ln_kernel(o_ref, x_ref, w_ref, g_ref, b_ref, y_ref, *, alpha):
    m = _dot(o_ref[...], w_ref[...])
    y_ref[...] = _layernorm(alpha * x_ref[...] + m, g_ref[...], b_ref[...])


def _outproj_ln(o_bf16, x, w_bf16, g, b, alpha):
    m, d = x.shape
    bm = min(ROW_BLOCK, m)
    row = pl.BlockSpec((bm, d), lambda i: (i, 0))
    vec = pl.BlockSpec((1, d), lambda i: (0, 0))
    return pl.pallas_call(
        functools.partial(_outproj_ln_kernel, alpha=alpha),
        grid=(m // bm,),
        in_specs=[row, row, pl.BlockSpec((d, d), lambda i: (0, 0)), vec, vec],
        out_specs=row,
        out_shape=jax.ShapeDtypeStruct((m, d), F32),
        compiler_params=_params("parallel"),
        name="outproj_ln",
    )(o_bf16, x, w_bf16, g.reshape(1, d), b.reshape(1, d))


def _mlp_ln_kernel(x_ref, wu_ref, wd_ref, g_ref, b_ref, y_ref, acc_ref, *, alpha):
    c = pl.program_id(1)
    x = x_ref[...]
    h = jnp.maximum(_dot(x.astype(BF16), wu_ref[...]), 0.0)
    part = _dot((h * h).astype(BF16), wd_ref[...])

    @pl.when(c == 0)
    def _():
        acc_ref[...] = part

    @pl.when(c > 0)
    def _():
        acc_ref[...] += part

    @pl.when(c == pl.num_programs(1) - 1)
    def _():
        y_ref[...] = _layernorm(alpha * x + acc_ref[...], g_ref[...], b_ref[...])


def _mlp_ln(x, wu_bf16, wd_bf16, g, b, alpha):
    m, d = x.shape
    dff = wu_bf16.shape[1]
    bm = min(ROW_BLOCK, m)
    fc = min(FF_CHUNK, dff)
    row = pl.BlockSpec((bm, d), lambda i, c: (i, 0))
    vec = pl.BlockSpec((1, d), lambda i, c: (0, 0))
    return pl.pallas_call(
        functools.partial(_mlp_ln_kernel, alpha=alpha),
        grid=(m // bm, dff // fc),
        in_specs=[row,
                  pl.BlockSpec((d, fc), lambda i, c: (0, c)),
                  pl.BlockSpec((fc, d), lambda i, c: (c, 0)),
                  vec, vec],
        out_specs=row,
        out_shape=jax.ShapeDtypeStruct((m, d), F32),
        scratch_shapes=[pltpu.VMEM((bm, d), F32)],
        compiler_params=_params("parallel", "arbitrary"),
        name="mlp_ln",
    )(x, wu_bf16, wd_bf16, g.reshape(1, d), b.reshape(1, d))


def _sb_scores(qh, kt, mask, nls_ref, nls0_ref, logb_ref):
    z = _dot(qh, kt)
    nls = _softplus2(z)
    logb = z - nls
    if mask is not None:
        nls = jnp.where(mask, nls, 0.0)
        logb = jnp.where(mask, logb, NEG_BIG)
    nls_ref[...] = nls.astype(BF16)
    for ch in range(nls.shape[1] // MXU_DIM):
        nls0_ref[ch] = _rep(_bf16_round(nls[:, ch * MXU_DIM:ch * MXU_DIM + 1]))
    logb_ref[...] = logb


def _sb_weights(nls_ref, nls0_ref, logb_ref, vb, u, acc_ref, c_ref):
    t = nls_ref.shape[0]
    n_chunks = nls_ref.shape[1] // MXU_DIM
    later = jnp.zeros((t, LANES), F32)
    a = [None] * n_chunks
    for ch in reversed(range(n_chunks)):
        cols = slice(ch * MXU_DIM, (ch + 1) * MXU_DIM)
        after = _dot(nls_ref[:, cols], u)
        blocks = []
        for lb in range(MXU_DIM // LANES):
            lanes = slice(lb * LANES, (lb + 1) * LANES)
            e = logb_ref[:, cols][:, lanes] + after[:, lanes]
            blocks.append(jnp.exp2(e if ch == n_chunks - 1 else e + later).astype(BF16))
        a[ch] = jnp.concatenate(blocks, axis=1)
        later = later + (_rep(after[:, 0:1]) - nls0_ref[ch])
    o = _dot(jnp.concatenate(a, axis=1), vb)
    c = c_ref[...]
    acc_ref[...] += jnp.exp2(c) * o
    c_ref[...] = c + later


def _sb_prompt_kernel(u_ref, q_ref, kt_ref, v_ref, o_ref, qs_ref, nls_ref, nls0_ref, logb_ref, acc_ref, c_ref):
    t = q_ref.shape[1]
    i = pl.program_id(2)
    qs_ref[0], qs_ref[1] = _split_halves(q_ref[0])
    acc_ref[...] = jnp.zeros_like(acc_ref)
    c_ref[...] = jnp.zeros_like(c_ref)

    def scores(j, slot, mask=None):
        kt = kt_ref[0, :, pl.ds(pl.multiple_of(j * t, t), t)]
        for h in range(2):
            _sb_scores(qs_ref[h], kt, mask, nls_ref.at[slot, h], nls0_ref.at[slot, h], logb_ref.at[slot, h])

    def weights(j, slot):
        vb = v_ref[0, pl.ds(pl.multiple_of(j * t, t), t), :]
        for h in range(2):
            _sb_weights(nls_ref.at[slot, h], nls0_ref.at[slot, h], logb_ref.at[slot, h], vb, u_ref[...],
                        acc_ref.at[h], c_ref.at[h])

    r = lax.broadcasted_iota(jnp.int32, (t, t), 0)
    c = lax.broadcasted_iota(jnp.int32, (t, t), 1)
    scores(i, 0, c < r)

    def stick_left():
        return jnp.max(c_ref[...]) > F32_EXP2_ZERO

    def body(state):
        s, _ = state
        j = i - 1 - 2 * s
        scores(j, 1)
        weights(j + 1, 0)
        scores(j - 1, 0)
        weights(j, 1)
        return s + 1, stick_left()

    _, alive = lax.while_loop(lambda state: (state[0] < i // 2) & state[1], body, (jnp.int32(0), True))

    @pl.when(alive & (i % 2 == 1))
    def _():
        scores(0, 1)
        weights(1, 0)
        weights(0, 1)

    @pl.when(alive & (i % 2 == 0))
    def _():
        weights(0, 0)

    lane_o = lax.broadcasted_iota(jnp.int32, (t, LANES), 1)
    o_ref[0] = jnp.where(lane_o < HEAD_DIM, acc_ref[0], acc_ref[1]).astype(o_ref.dtype)


def _sb_prompt_attention(q, kt, v, batch, seq):
    d = q.shape[1]
    t = min(ATT_BLOCK, seq)
    q3, v3 = (a.reshape(batch, seq, d) for a in (q, v))
    kt_spec = pl.BlockSpec((1, LANES, seq), lambda b, h, i: (b, h, 0))
    v_spec = pl.BlockSpec((1, seq, LANES), lambda b, h, i: (b, 0, h))
    q_spec = pl.BlockSpec((1, t, LANES), lambda b, h, i: (b, i, h))
    out = pl.pallas_call(
        _sb_prompt_kernel,
        grid=(batch, d // LANES, seq // t),
        in_specs=[pl.BlockSpec((MXU_DIM, MXU_DIM), lambda b, h, i: (0, 0)), q_spec, kt_spec, v_spec],
        out_specs=q_spec,
        out_shape=jax.ShapeDtypeStruct((batch, seq, d), BF16),
        scratch_shapes=[pltpu.VMEM((2, t, LANES), BF16),
                        pltpu.VMEM((2, 2, t, t), BF16),
                        pltpu.VMEM((2, 2, t // MXU_DIM, t, LANES), F32),
                        pltpu.VMEM((2, 2, t, t), F32),
                        pltpu.VMEM((2, t, LANES), F32),
                        pltpu.VMEM((2, t, LANES), F32)],
        compiler_params=_params("parallel", "parallel", "arbitrary"),
        name="sb_prompt",
    )(jnp.asarray(_suffix_matrix_np(MXU_DIM), BF16), q3, kt, v3)
    return out.reshape(batch * seq, d)


def _t5_bucket_np(dist):
    n = np.maximum(dist, 0).astype(np.int32)
    max_exact = REL_BUCKETS // 2
    nf = np.maximum(n, 1).astype(np.float32)
    large = max_exact + (np.log(nf / np.float32(max_exact)) / np.float32(math.log(REL_MAX_DIST / max_exact))
                         * np.float32(REL_BUCKETS - max_exact)).astype(np.int32)
    large = np.minimum(large, REL_BUCKETS - 1)
    return np.where(n < max_exact, n, large).astype(np.int32)


def _bias_table_kernel(rel_ref, bsub_ref, bdiag_ref, bdec_ref, tab_ref, dec_ref, new_ref):
    h = pl.program_id(0)
    far = rel_ref[REL_BUCKETS - 1, h]

    def gather(bucket):
        out = jnp.zeros(bucket.shape, F32)
        for b in range(REL_BUCKETS):
            out = jnp.where(bucket == b, rel_ref[b, h], out)
        return (out - far) * LOG2E

    tab_ref[0, 0] = gather(bsub_ref[...])
    tab_ref[0, 1] = gather(bdiag_ref[...])
    dec_ref[0] = gather(bdec_ref[...])
    new_ref[0] = jnp.zeros(new_ref.shape[1:], F32) + (rel_ref[0, h] - far) * LOG2E


def _bias_tables(rel_bias, t):
    assert t >= REL_MAX_DIST and PAGE_SIZE >= REL_MAX_DIST
    n_heads = rel_bias.shape[1]
    r = np.arange(t)[:, None]
    c = np.arange(t)[None, :]
    b_sub = _t5_bucket_np(r - c + t)
    b_diag = _t5_bucket_np(r - c)
    b_dec = _t5_bucket_np(PAGE_SIZE - np.arange(PAGE_SIZE))[None, :]
    full = lambda shape: pl.BlockSpec(shape, lambda h: (0,) * len(shape))
    return pl.pallas_call(
        _bias_table_kernel,
        grid=(n_heads,),
        in_specs=[pl.BlockSpec(memory_space=pltpu.SMEM), full((t, t)), full((t, t)), full((1, PAGE_SIZE))],
        out_specs=[pl.BlockSpec((1, 2, t, t), lambda h: (h, 0, 0, 0)),
                   pl.BlockSpec((1, 1, PAGE_SIZE), lambda h: (h, 0, 0)),
                   pl.BlockSpec((1, 1, LANES), lambda h: (h, 0, 0))],
        out_shape=[jax.ShapeDtypeStruct((n_heads, 2, t, t), F32),
                   jax.ShapeDtypeStruct((n_heads, 1, PAGE_SIZE), F32),
                   jax.ShapeDtypeStruct((n_heads, 1, LANES), F32)],
        compiler_params=_params("arbitrary"),
        name="t5_bias_tables",
    )(rel_bias, jnp.asarray(b_sub), jnp.asarray(b_diag), jnp.asarray(b_dec))


def _df_lambda(lv_ref, lam_init):
    lv = lv_ref[...]
    a = jnp.sum(lv[0:1, :] * lv[1:2, :], axis=1, keepdims=True)
    b = jnp.sum(lv[2:3, :] * lv[3:4, :], axis=1, keepdims=True)
    return jnp.exp(a) - jnp.exp(b) + lam_init


def _df_prompt_kernel(lv_ref, g_ref, q_ref, kt_ref, v_ref, tab_ref, o_ref,
                      qs_ref, s_ref, rmax_ref, m_ref, l_ref, acc_ref, *, lam_init):
    t = q_ref.shape[1]
    i = pl.program_id(2)
    qs_ref[0], qs_ref[1] = _split_halves(q_ref[0])
    m_ref[...] = jnp.full_like(m_ref, NEG_BIG)
    l_ref[...] = jnp.zeros_like(l_ref)
    acc_ref[...] = jnp.zeros_like(acc_ref)

    def scores(j, slot, bias=None, mask=None):
        kt = kt_ref[0, :, pl.ds(pl.multiple_of(j * t, t), t)]
        for c in range(2):
            s = _dot(qs_ref[c], kt)
            if bias is not None:
                s = s + bias
            if mask is not None:
                s = jnp.where(mask, s, NEG_BIG)
            s_ref[slot, c] = s
            rmax_ref[slot, c] = _rep(jnp.max(s, axis=1, keepdims=True))

    def update(j, slot):
        vb = v_ref[0, pl.ds(pl.multiple_of(j * t, t), t), :]
        for c in range(2):
            m = m_ref[c]
            m_new = jnp.maximum(m, rmax_ref[slot, c])
            alpha = jnp.exp2(m - m_new)
            p = [jnp.exp2(sb - m_new) for sb in _lane_blocks(s_ref[slot, c])]
            psum = p[0]
            for pb in p[1:]:
                psum = psum + pb
            l_ref[c] = alpha * l_ref[c] + _rep(jnp.sum(psum, axis=1, keepdims=True))
            pv = _dot(jnp.concatenate([pb.astype(BF16) for pb in p], axis=1), vb)
            acc_ref[c] = alpha * acc_ref[c] + pv
            m_ref[c] = m_new

    r = lax.broadcasted_iota(jnp.int32, (t, t), 0)
    c = lax.broadcasted_iota(jnp.int32, (t, t), 1)
    scores(i, 0, tab_ref[0, 1], c <= r)

    @pl.when(i > 0)
    def _():
        scores(i - 1, 1, tab_ref[0, 0])
        update(i, 0)

    def body(s, _):
        j = i - 2 - 2 * s
        scores(j, 0)
        update(j + 1, 1)
        scores(j - 1, 1)
        update(j, 0)
        return 0

    n_far = jnp.maximum(i - 1, 0)
    lax.fori_loop(0, n_far // 2, body, 0)

    @pl.when(n_far % 2 == 1)
    def _():
        scores(0, 0)
        update(1, 1)
        update(0, 0)

    @pl.when((n_far % 2 == 0) & (i > 0))
    def _():
        update(0, 1)

    @pl.when(i == 0)
    def _():
        update(0, 0)

    lam = _df_lambda(lv_ref, lam_init)
    o = acc_ref[0] / l_ref[0] - lam * (acc_ref[1] / l_ref[1])
    o = o * lax.rsqrt(jnp.mean(o * o, axis=1, keepdims=True) + 1e-5)
    o_ref[0] = (o * g_ref[...] * (1.0 - lam_init)).astype(o_ref.dtype)


def _df_prompt_attention(q, kt, v, tab, lam_vec, subln_g, batch, seq, lam_init):
    d = q.shape[1]
    t = tab.shape[2]
    q3, v3 = (a.reshape(batch, seq, d) for a in (q, v))
    kt_spec = pl.BlockSpec((1, LANES, seq), lambda b, h, i: (b, h, 0))
    v_spec = pl.BlockSpec((1, seq, LANES), lambda b, h, i: (b, 0, h))
    q_spec = pl.BlockSpec((1, t, LANES), lambda b, h, i: (b, i, h))
    out = pl.pallas_call(
        functools.partial(_df_prompt_kernel, lam_init=lam_init),
        grid=(batch, d // LANES, seq // t),
        in_specs=[pl.BlockSpec(lam_vec.shape, lambda b, h, i: (0, 0)),
                  pl.BlockSpec((1, LANES), lambda b, h, i: (0, 0)),
                  q_spec, kt_spec, v_spec,
                  pl.BlockSpec((1, 2, t, t), lambda b, h, i: (h, 0, 0, 0))],
        out_specs=q_spec,
        out_shape=jax.ShapeDtypeStruct((batch, seq, d), BF16),
        scratch_shapes=[pltpu.VMEM((2, t, LANES), BF16),
                        pltpu.VMEM((2, 2, t, t), F32),
                        pltpu.VMEM((2, 2, t, LANES), F32),
                        pltpu.VMEM((2, t, LANES), F32),
                        pltpu.VMEM((2, t, LANES), F32),
                        pltpu.VMEM((2, t, LANES), F32)],
        compiler_params=_params("parallel", "parallel", "arbitrary"),
        name="df_prompt",
    )(lam_vec, subln_g.reshape(1, LANES), q3, kt, v3, tab)
    return out.reshape(batch * seq, d)


def _head_rows(q_row, col_of_row):
    d = q_row.shape[1]
    q = jnp.broadcast_to(q_row.astype(F32), (DEC_HEADS, d))
    row = lax.broadcasted_iota(jnp.int32, (DEC_HEADS, d), 0)
    col = lax.broadcasted_iota(jnp.int32, (DEC_HEADS, d), 1)
    start = col_of_row(row)
    return jnp.where((col >= start) & (col < start + HEAD_DIM), q, 0.0).astype(BF16)


def _sb_decode_kernel(pt_ref, q_ref, *refs, first_page, n_pages):
    kt_hbm, vt_hbm, o_ref, kbuf, vbuf, sem, acc_ref = refs
    b = pl.program_id(0)

    def page_copies(seq, p, slot):
        page = first_page + pt_ref[seq, n_pages - 1 - p]
        return (pltpu.make_async_copy(kt_hbm.at[page], kbuf.at[slot], sem.at[0, slot]),
                pltpu.make_async_copy(vt_hbm.at[page], vbuf.at[slot], sem.at[1, slot]))

    @pl.when(b == 0)
    def _():
        for cp in page_copies(b, 0, 0):
            cp.start()

    acc_ref[...] = jnp.zeros_like(acc_ref)
    qrows = _head_rows(q_ref[0], lambda r: r * HEAD_DIM)
    u = _suffix_matrix(PAGE_SIZE)

    def body(state):
        p, cc, _ = state
        slot = p % 2

        @pl.when(p + 1 < n_pages)
        def _():
            for cp in page_copies(b, p + 1, 1 - slot):
                cp.start()

        for cp in page_copies(b, p, slot):
            cp.wait()
        z = _dot(qrows, kbuf[slot].astype(BF16))
        nls = _softplus2(z)
        after = _dot(nls.astype(BF16), u)
        a = jnp.exp2(z - nls + after + cc)
        for h in range(DEC_HEADS):
            rows = slice(h * HEAD_DIM, (h + 1) * HEAD_DIM)
            acc_ref[rows, :] += vbuf[slot, rows, :] * a[h:h + 1, :]
        cc = cc + after[:, 0:1] - _bf16_round(nls[:, 0:1])
        return p + 1, cc, jnp.max(cc) > F32_EXP2_ZERO

    start = (jnp.int32(0), jnp.zeros((DEC_HEADS, 1), F32), True)
    p_end, _, _ = lax.while_loop(lambda state: (state[0] < n_pages) & state[2], body, start)

    @pl.when(p_end < n_pages)
    def _():
        for cp in page_copies(b, p_end, p_end % 2):
            cp.wait()

    @pl.when(b + 1 < pl.num_programs(0))
    def _():
        for cp in page_copies(b + 1, 0, 0):
            cp.start()

    o_ref[0] = jnp.sum(acc_ref[...].T, axis=0, keepdims=True).astype(o_ref.dtype)


def _page_specs(n_pages, npg, block, first_page):
    return [pl.BlockSpec((1,) + block, lambda b, s, pt, p=p: (first_page + pt[b, s * npg + p], 0, 0))
            for p in range(npg)]


def _sb_decode_attention(q, cache_kt, cache_vt, page_table, first_page):
    db, d = q.shape
    row = pl.BlockSpec((1, 1, d), lambda b, pt: (b, 0, 0))
    in_hbm = pl.BlockSpec(memory_space=pl.ANY)
    out = pl.pallas_call(
        functools.partial(_sb_decode_kernel, first_page=first_page, n_pages=page_table.shape[1]),
        grid_spec=pltpu.PrefetchScalarGridSpec(
            num_scalar_prefetch=1,
            grid=(db,),
            in_specs=[row, in_hbm, in_hbm],
            out_specs=row,
            scratch_shapes=[pltpu.VMEM((2, d, PAGE_SIZE), F32),
                            pltpu.VMEM((2, d, PAGE_SIZE), F32),
                            pltpu.SemaphoreType.DMA((2, 2)),
                            pltpu.VMEM((d, PAGE_SIZE), F32)]),
        out_shape=jax.ShapeDtypeStruct((db, 1, d), BF16),
        compiler_params=_params("arbitrary"),
        name="sb_decode",
    )(page_table, q.reshape(db, 1, d), cache_kt, cache_vt)
    return out.reshape(db, d)


def _df_decode_kernel(pt_ref, lv_ref, g_ref, dec_ref, new_ref, spread_ref, q_ref, kn_ref, vn_ref, *refs, lam_init):
    del pt_ref
    npg = (len(refs) - 4) // 2
    kt_refs, v_refs = refs[:npg], refs[npg:2 * npg]
    o_ref, acc_ref, m_ref, l_ref = refs[2 * npg:]
    s = pl.program_id(1)
    last = s == pl.num_programs(1) - 1
    n_heads = DEC_HEADS // 2

    @pl.when(s == 0)
    def _():
        acc_ref[...] = jnp.zeros_like(acc_ref)
        m_ref[...] = jnp.full_like(m_ref, NEG_BIG)
        l_ref[...] = jnp.zeros_like(l_ref)

    qrows = _head_rows(q_ref[0], lambda r: (r % n_heads) * LANES + (r // n_heads) * HEAD_DIM)
    scores = []
    for p in range(npg):
        sp = _dot(qrows, kt_refs[p][0].astype(BF16))
        if p == npg - 1:
            sp = sp + jnp.where(last, dec_ref[...], 0.0)
        scores.append(sp)
    m_old = m_ref[...]
    m_new = m_old
    for sp in scores:
        m_new = jnp.maximum(m_new, jnp.max(sp, axis=1, keepdims=True))
    alpha = jnp.exp2(m_old - m_new)
    l = alpha * l_ref[...]
    acc = alpha * acc_ref[...]
    row = lax.broadcasted_iota(jnp.int32, (DEC_HEADS, PAGE_SIZE * n_heads), 0)
    lane = lax.broadcasted_iota(jnp.int32, (DEC_HEADS, PAGE_SIZE * n_heads), 1)
    own_head = lane % n_heads == row % n_heads
    for p, sp in enumerate(scores):
        pr = jnp.exp2(sp - m_new)
        l = l + jnp.sum(pr, axis=1, keepdims=True)
        spread = _dot(pr.astype(BF16), spread_ref[...])
        spread = jnp.where(own_head, spread, 0.0).astype(BF16)
        acc = acc + _dot(spread, v_refs[p][0].astype(BF16))
    m_ref[...] = m_new
    l_ref[...] = l
    acc_ref[...] = acc

    @pl.when(last)
    def _():
        kn = kn_ref[0].astype(F32)
        s_new = jnp.sum(qrows.astype(F32) * kn, axis=1, keepdims=True) + new_ref[...][:, 0:1]
        m_fin = jnp.maximum(m_new, s_new)
        a_old = jnp.exp2(m_new - m_fin)
        p_new = jnp.exp2(s_new - m_fin)
        l_fin = a_old * l + p_new
        vn = vn_ref[0]
        acc_fin = a_old * acc + p_new * jnp.concatenate([vn, vn], axis=0)
        norm = acc_fin / l_fin
        lam = _df_lambda(lv_ref, lam_init)
        o = norm[0:n_heads] - lam * norm[n_heads:DEC_HEADS]
        o = o * lax.rsqrt(jnp.mean(o * o, axis=1, keepdims=True) + 1e-5)
        o_ref[0] = (o * g_ref[...] * (1.0 - lam_init)).astype(o_ref.dtype)


def _df_decode_attention(q, k_new, v_new, cache_kt, cache_v, page_table, first_page, dec_bias, new_bias,
                         lam_vec, subln_g, lam_init):
    db, d = q.shape
    n_pages = page_table.shape[1]
    npg = min(DEC_PAGES_PER_STEP, n_pages)
    n_heads = d // LANES
    row = pl.BlockSpec((1, 1, d), lambda b, s, pt: (b, 0, 0))
    head_rows = pl.BlockSpec((1, n_heads, LANES), lambda b, s, pt: (b, 0, 0))
    const = lambda shape: pl.BlockSpec(shape, lambda b, s, pt: (0,) * len(shape))
    kt_pages = _page_specs(n_pages, npg, (d, PAGE_SIZE), first_page)
    v_pages = _page_specs(n_pages, npg, (PAGE_SIZE * n_heads, LANES), first_page)
    spread = np.repeat(np.eye(PAGE_SIZE), n_heads, axis=1)
    out = pl.pallas_call(
        functools.partial(_df_decode_kernel, lam_init=lam_init),
        grid_spec=pltpu.PrefetchScalarGridSpec(
            num_scalar_prefetch=1,
            grid=(db, n_pages // npg),
            in_specs=[const(lam_vec.shape), const((1, LANES)), const((DEC_HEADS, PAGE_SIZE)),
                      const((DEC_HEADS, LANES)), const(spread.shape), row, row, head_rows] + kt_pages + v_pages,
            out_specs=head_rows,
            scratch_shapes=[pltpu.VMEM((DEC_HEADS, LANES), F32), pltpu.VMEM((DEC_HEADS, 1), F32),
                            pltpu.VMEM((DEC_HEADS, 1), F32)]),
        out_shape=jax.ShapeDtypeStruct((db, n_heads, LANES), BF16),
        compiler_params=_params("parallel", "arbitrary"),
        name="df_decode",
    )(page_table, lam_vec, subln_g.reshape(1, LANES), dec_bias, new_bias, jnp.asarray(spread, BF16),
      q.reshape(db, 1, d), k_new.reshape(db, 1, d), v_new.reshape(db, n_heads, LANES),
      *([cache_kt] * npg), *([cache_v] * npg))
    return out.reshape(db, d)


def kernel(x_prompt, x_sample, cache_sb_k, cache_sb_v, cache_df_k, cache_df_v, page_table,
           w_in, w_out, ln_g, ln_b, df_lambda, df_subln_g, rel_bias, w_up, w_down):
    batch, seq, d = x_prompt.shape
    db = x_sample.shape[0]
    depth = w_in.shape[0]
    n_sb_heads = d // HEAD_DIM
    n_df_heads = d // (2 * HEAD_DIM)
    alpha = (2 * depth) ** 0.25
    q_scale = ATT_SCALE * LOG2E
    t = min(ATT_BLOCK, seq)
    assert x_sample.shape[1] == 1 and d // HEAD_DIM == DEC_HEADS
    assert cache_sb_k.shape[2] == PAGE_SIZE and seq % t == 0 and t % MXU_DIM == 0

    xp = x_prompt.reshape(batch * seq, d)
    xs = x_sample.reshape(db, d)
    w_in_b, w_out_b = w_in.astype(BF16), w_out.astype(BF16)
    w_up_b, w_down_b = w_up.astype(BF16), w_down.astype(BF16)
    n_pool = cache_sb_k.shape[1]
    feature_major = lambda cache: jnp.moveaxis(cache, 2, -1).reshape(-1, d, PAGE_SIZE)
    df_v_pages = cache_df_v.reshape(-1, PAGE_SIZE * n_df_heads, 2 * HEAD_DIM)
    token_major = lambda xt, feat: jnp.moveaxis(xt.reshape((batch,) + feat + (seq,)), -1, 1)

    sb_kp, sb_vp, sb_ks, sb_vs = [], [], [], []
    df_kp, df_vp, df_ks, df_vs = [], [], [], []
    for i in range(depth):
        j = i // 2
        qp, kpt, kptb, vp, vpb = _qkv_proj_fm(xp, w_in_b[i], q_scale, batch, v_feature_major=(i % 2 == 0))
        qs, ks_, vs_ = _qkv_proj(xs, w_in_b[i], q_scale)
        if i % 2 == 0:
            op = _sb_prompt_attention(qp, kptb, vpb, batch, seq)
            os_ = _sb_decode_attention(qs, feature_major(cache_sb_k), feature_major(cache_sb_v),
                                       page_table, j * n_pool)
            sb_kp.append(token_major(kpt, (n_sb_heads, HEAD_DIM)))
            sb_vp.append(token_major(vp, (n_sb_heads, HEAD_DIM)))
            sb_ks.append(ks_.reshape(db, 1, n_sb_heads, HEAD_DIM))
            sb_vs.append(vs_.reshape(db, 1, n_sb_heads, HEAD_DIM))
        else:
            lam_init = 0.8 - 0.6 * math.exp(-0.3 * i)
            tab, dec_bias, new_bias = _bias_tables(rel_bias, t)
            dec_bias = jnp.tile(dec_bias.reshape(n_df_heads, PAGE_SIZE), (2, 1))
            new_bias = jnp.tile(new_bias.reshape(n_df_heads, LANES), (2, 1))
            op = _df_prompt_attention(qp, kptb, vpb, tab, df_lambda[j], df_subln_g[j], batch, seq, lam_init)
            os_ = _df_decode_attention(qs, ks_, vs_, feature_major(cache_df_k), df_v_pages, page_table, j * n_pool,
                                       dec_bias, new_bias, df_lambda[j], df_subln_g[j], lam_init)
            df_kp.append(token_major(kpt, (n_df_heads, 2, HEAD_DIM)))
            df_vp.append(vp.reshape(batch, seq, n_df_heads, 2 * HEAD_DIM))
            df_ks.append(ks_.reshape(db, 1, n_df_heads, 2, HEAD_DIM))
            df_vs.append(vs_.reshape(db, 1, n_df_heads, 2 * HEAD_DIM))
        xp = _outproj_ln(op, xp, w_out_b[i], ln_g[i, 0], ln_b[i, 0], alpha)
        xs = _outproj_ln(os_, xs, w_out_b[i], ln_g[i, 0], ln_b[i, 0], alpha)
        xp = _mlp_ln(xp, w_up_b[i], w_down_b[i], ln_g[i, 1], ln_b[i, 1], alpha)
        xs = _mlp_ln(xs, w_up_b[i], w_down_b[i], ln_g[i, 1], ln_b[i, 1], alpha)
    return (xp.reshape(batch, seq, d), xs.reshape(db, 1, d),
            jnp.stack(sb_kp), jnp.stack(sb_vp), jnp.stack(sb_ks), jnp.stack(sb_vs),
            jnp.stack(df_kp), jnp.stack(df_vp), jnp.stack(df_ks), jnp.stack(df_vs))
```

```python
import functools
import math

import numpy as np
import jax
import jax.numpy as jnp
from jax import lax
from jax.experimental import pallas as pl
from jax.experimental.pallas import tpu as pltpu

HEAD_DIM = 64
PAGE_SIZE = 128
REL_BUCKETS = 32
REL_MAX_DIST = 128
ATT_SCALE = HEAD_DIM ** -0.5
NEG_BIG = -1e30
LN_EPS = 1e-5
LOG2E = math.log2(math.e)
F32_EXP2_ZERO = -150.0

LANES = 128
MXU_DIM = 256
VMEM_LIMIT_BYTES = 48 * 1024 * 1024

ATT_BLOCK = 2 * MXU_DIM
ROW_BLOCK = 512
FF_CHUNK = 4096
DEC_HEADS = 16
DEC_PAGES_PER_STEP = 16

F32 = jnp.float32
BF16 = jnp.bfloat16


def _params(*semantics):
    return pltpu.CompilerParams(dimension_semantics=semantics,
                                vmem_limit_bytes=VMEM_LIMIT_BYTES)


def _dot(a, b):
    return jnp.dot(a, b, preferred_element_type=F32)


def _dot_nt(a, b):
    return lax.dot_general(a, b, (((1,), (1,)), ((), ())), preferred_element_type=F32)


def _suffix_matrix_np(n):
    return np.where(np.arange(n)[:, None] > np.arange(n)[None, :], -1.0, 0.0)


def _suffix_matrix(n):
    r = lax.broadcasted_iota(jnp.int32, (n, n), 0)
    c = lax.broadcasted_iota(jnp.int32, (n, n), 1)
    return jnp.where(r > c, -1.0, 0.0).astype(BF16)


def _split_halves(q2):
    qf = q2.astype(F32)
    first = lax.broadcasted_iota(jnp.int32, qf.shape, 1) < HEAD_DIM
    return jnp.where(first, qf, 0.0).astype(BF16), jnp.where(first, 0.0, qf).astype(BF16)


def _bf16_round(x):
    return x.astype(BF16).astype(F32)


def _rep(col):
    return jnp.broadcast_to(col, (col.shape[0], LANES))


def _lane_blocks(x):
    return [x[:, cb * LANES:(cb + 1) * LANES] for cb in range(x.shape[1] // LANES)]


def _softplus2(z):
    neg_abs = pltpu.bitcast(pltpu.bitcast(z, jnp.uint32) | jnp.uint32(0x80000000), F32)
    return jnp.maximum(z, 0.0) + jnp.log2(1.0 + jnp.exp2(neg_abs))


def _proj_kernel(x_ref, w_ref, q_ref, k_ref, v_ref, *, q_scale):
    d = x_ref.shape[1]
    x = x_ref[...].astype(BF16)
    q_ref[...] = (_dot(x, w_ref[:, 0:d]) * q_scale).astype(BF16)
    k_ref[...] = _dot(x, w_ref[:, d:2 * d])
    v_ref[...] = _dot(x, w_ref[:, 2 * d:3 * d])


def _qkv_proj(x, w_bf16, q_scale):
    m, d = x.shape
    bm = min(ROW_BLOCK, m)
    row = pl.BlockSpec((bm, d), lambda i: (i, 0))
    return pl.pallas_call(
        functools.partial(_proj_kernel, q_scale=q_scale),
        grid=(m // bm,),
        in_specs=[row, pl.BlockSpec((d, 3 * d), lambda i: (0, 0))],
        out_specs=[row, row, row],
        out_shape=[jax.ShapeDtypeStruct((m, d), BF16),
                   jax.ShapeDtypeStruct((m, d), F32),
                   jax.ShapeDtypeStruct((m, d), F32)],
        compiler_params=_params("parallel"),
        name="qkv_proj",
    )(x, w_bf16)


def _proj_fm_kernel(x_ref, w_ref, q_ref, kt_ref, ktb_ref, v_ref, vb_ref, *, q_scale, v_feature_major):
    d = x_ref.shape[1]
    x = x_ref[...].astype(BF16)
    q_ref[...] = (_dot(x, w_ref[:, 0:d]) * q_scale).astype(BF16)
    kt = _dot(x, w_ref[:, d:2 * d]).T
    kt_ref[0] = kt
    ktb_ref[0] = kt.astype(BF16)
    v = _dot(x, w_ref[:, 2 * d:3 * d])
    vb_ref[...] = v.astype(BF16)
    if v_feature_major:
        v_ref[0] = v.T
    else:
        v_ref[...] = v


def _qkv_proj_fm(x, w_bf16, q_scale, batch, v_feature_major):
    m, d = x.shape
    seq = m // batch
    bm = min(ROW_BLOCK, seq)
    nb = seq // bm
    row = pl.BlockSpec((bm, d), lambda i: (i, 0))
    fm = pl.BlockSpec((1, d, bm), lambda i: (i // nb, 0, i % nb))
    fm_f32 = jax.ShapeDtypeStruct((batch, d, seq), F32)
    return pl.pallas_call(
        functools.partial(_proj_fm_kernel, q_scale=q_scale, v_feature_major=v_feature_major),
        grid=(m // bm,),
        in_specs=[row, pl.BlockSpec((d, 3 * d), lambda i: (0, 0))],
        out_specs=[row, fm, fm, fm if v_feature_major else row, row],
        out_shape=[jax.ShapeDtypeStruct((m, d), BF16),
                   fm_f32,
                   jax.ShapeDtypeStruct((batch, d, seq), BF16),
                   fm_f32 if v_feature_major else jax.ShapeDtypeStruct((m, d), F32),
                   jax.ShapeDtypeStruct((m, d), BF16)],
        compiler_params=_params("parallel"),
        name="qkv_proj_fm",
    )(x, w_bf16)


def _layernorm(xf, g, b):
    mu = jnp.mean(xf, axis=-1, keepdims=True)
    xc = xf - mu
    var = jnp.mean(xc * xc, axis=-1, keepdims=True)
    return xc * lax.rsqrt(var + LN_EPS) * g + b


def _outproj_ln_kernel(o_ref, x_ref, w_ref, g_ref, b_ref, y_ref, *, alpha):
    m = _dot(o_ref[...], w_ref[...])
    y_ref[...] = _layernorm(alpha * x_ref[...] + m, g_ref[...], b_ref[...])


def _outproj_ln(o_bf16, x, w_bf16, g, b, alpha):
    m, d = x.shape
    bm = min(ROW_BLOCK, m)
    row = pl.BlockSpec((bm, d), lambda i: (i, 0))
    vec = pl.BlockSpec((1, d), lambda i: (0, 0))
    return pl.pallas_call(
        functools.partial(_outproj_ln_kernel, alpha=alpha),
        grid=(m // bm,),
        in_specs=[row, row, pl.BlockSpec((d, d), lambda i: (0, 0)), vec, vec],
        out_specs=row,
        out_shape=jax.ShapeDtypeStruct((m, d), F32),
        compiler_params=_params("parallel"),
        name="outproj_ln",
    )(o_bf16, x, w_bf16, g.reshape(1, d), b.reshape(1, d))


def _mlp_ln_kernel(x_ref, wu_ref, wd_ref, g_ref, b_ref, y_ref, acc_ref, *, alpha):
    c = pl.program_id(1)
    x = x_ref[...]
    h = jnp.maximum(_dot(x.astype(BF16), wu_ref[...]), 0.0)
    part = _dot((h * h).astype(BF16), wd_ref[...])

    @pl.when(c == 0)
    def _():
        acc_ref[...] = part

    @pl.when(c > 0)
    def _():
        acc_ref[...] += part

    @pl.when(c == pl.num_programs(1) - 1)
    def _():
        y_ref[...] = _layernorm(alpha * x + acc_ref[...], g_ref[...], b_ref[...])


def _mlp_ln(x, wu_bf16, wd_bf16, g, b, alpha):
    m, d = x.shape
    dff = wu_bf16.shape[1]
    bm = min(ROW_BLOCK, m)
    fc = min(FF_CHUNK, dff)
    row = pl.BlockSpec((bm, d), lambda i, c: (i, 0))
    vec = pl.BlockSpec((1, d), lambda i, c: (0, 0))
    return pl.pallas_call(
        functools.partial(_mlp_ln_kernel, alpha=alpha),
        grid=(m // bm, dff // fc),
        in_specs=[row,
                  pl.BlockSpec((d, fc), lambda i, c: (0, c)),
                  pl.BlockSpec((fc, d), lambda i, c: (c, 0)),
                  vec, vec],
        out_specs=row,
        out_shape=jax.ShapeDtypeStruct((m, d), F32),
        scratch_shapes=[pltpu.VMEM((bm, d), F32)],
        compiler_params=_params("parallel", "arbitrary"),
        name="mlp_ln",
    )(x, wu_bf16, wd_bf16, g.reshape(1, d), b.reshape(1, d))


def _sb_scores(qh, kt, mask, nls_ref, nls0_ref, logb_ref):
    z = _dot(qh, kt)
    nls = _softplus2(z)
    logb = z - nls
    if mask is not None:
        nls = jnp.where(mask, nls, 0.0)
        logb = jnp.where(mask, logb, NEG_BIG)
    nls_ref[...] = nls.astype(BF16)
    for ch in range(nls.shape[1] // MXU_DIM):
        nls0_ref[ch] = _rep(_bf16_round(nls[:, ch * MXU_DIM:ch * MXU_DIM + 1]))
    logb_ref[...] = logb


def _sb_weights(nls_ref, nls0_ref, logb_ref, vb, u, acc_ref, c_ref):
    t = nls_ref.shape[0]
    n_chunks = nls_ref.shape[1] // MXU_DIM
    later = jnp.zeros((t, LANES), F32)
    a = [None] * n_chunks
    for ch in reversed(range(n_chunks)):
        cols = slice(ch * MXU_DIM, (ch + 1) * MXU_DIM)
        after = _dot(nls_ref[:, cols], u)
        blocks = []
        for lb in range(MXU_DIM // LANES):
            lanes = slice(lb * LANES, (lb + 1) * LANES)
            e = logb_ref[:, cols][:, lanes] + after[:, lanes]
            blocks.append(jnp.exp2(e if ch == n_chunks - 1 else e + later).astype(BF16))
        a[ch] = jnp.concatenate(blocks, axis=1)
        later = later + (_rep(after[:, 0:1]) - nls0_ref[ch])
    o = _dot(jnp.concatenate(a, axis=1), vb)
    c = c_ref[...]
    acc_ref[...] += jnp.exp2(c) * o
    c_ref[...] = c + later


def _sb_prompt_kernel(u_ref, q_ref, kt_ref, v_ref, o_ref, qs_ref, nls_ref, nls0_ref, logb_ref, acc_ref, c_ref):
    t = q_ref.shape[1]
    i = pl.program_id(2)
    qs_ref[0], qs_ref[1] = _split_halves(q_ref[0])
    acc_ref[...] = jnp.zeros_like(acc_ref)
    c_ref[...] = jnp.zeros_like(c_ref)

    def scores(j, slot, mask=None):
        kt = kt_ref[0, :, pl.ds(pl.multiple_of(j * t, t), t)]
        for h in range(2):
            _sb_scores(qs_ref[h], kt, mask, nls_ref.at[slot, h], nls0_ref.at[slot, h], logb_ref.at[slot, h])

    def weights(j, slot):
        vb = v_ref[0, pl.ds(pl.multiple_of(j * t, t), t), :]
        for h in range(2):
            _sb_weights(nls_ref.at[slot, h], nls0_ref.at[slot, h], logb_ref.at[slot, h], vb, u_ref[...],
                        acc_ref.at[h], c_ref.at[h])

    r = lax.broadcasted_iota(jnp.int32, (t, t), 0)
    c = lax.broadcasted_iota(jnp.int32, (t, t), 1)
    scores(i, 0, c < r)

    def stick_left():
        return jnp.max(c_ref[...]) > F32_EXP2_ZERO

    def body(state):
        s, _ = state
        j = i - 1 - 2 * s
        scores(j, 1)
        weights(j + 1, 0)
        scores(j - 1, 0)
        weights(j, 1)
        return s + 1, stick_left()

    _, alive = lax.while_loop(lambda state: (state[0] < i // 2) & state[1], body, (jnp.int32(0), True))

    @pl.when(alive & (i % 2 == 1))
    def _():
        scores(0, 1)
        weights(1, 0)
        weights(0, 1)

    @pl.when(alive & (i % 2 == 0))
    def _():
        weights(0, 0)

    lane_o = lax.broadcasted_iota(jnp.int32, (t, LANES), 1)
    o_ref[0] = jnp.where(lane_o < HEAD_DIM, acc_ref[0], acc_ref[1]).astype(o_ref.dtype)


def _sb_prompt_attention(q, kt, v, batch, seq):
    d = q.shape[1]
    t = min(ATT_BLOCK, seq)
    q3, v3 = (a.reshape(batch, seq, d) for a in (q, v))
    kt_spec = pl.BlockSpec((1, LANES, seq), lambda b, h, i: (b, h, 0))
    v_spec = pl.BlockSpec((1, seq, LANES), lambda b, h, i: (b, 0, h))
    q_spec = pl.BlockSpec((1, t, LANES), lambda b, h, i: (b, i, h))
    out = pl.pallas_call(
        _sb_prompt_kernel,
        grid=(batch, d // LANES, seq // t),
        in_specs=[pl.BlockSpec((MXU_DIM, MXU_DIM), lambda b, h, i: (0, 0)), q_spec, kt_spec, v_spec],
        out_specs=q_spec,
        out_shape=jax.ShapeDtypeStruct((batch, seq, d), BF16),
        scratch_shapes=[pltpu.VMEM((2, t, LANES), BF16),
                        pltpu.VMEM((2, 2, t, t), BF16),
                        pltpu.VMEM((2, 2, t // MXU_DIM, t, LANES), F32),
                        pltpu.VMEM((2, 2, t, t), F32),
                        pltpu.VMEM((2, t, LANES), F32),
                        pltpu.VMEM((2, t, LANES), F32)],
        compiler_params=_params("parallel", "parallel", "arbitrary"),
        name="sb_prompt",
    )(jnp.asarray(_suffix_matrix_np(MXU_DIM), BF16), q3, kt, v3)
    return out.reshape(batch * seq, d)


def _t5_bucket_np(dist):
    n = np.maximum(dist, 0).astype(np.int32)
    max_exact = REL_BUCKETS // 2
    nf = np.maximum(n, 1).astype(np.float32)
    large = max_exact + (np.log(nf / np.float32(max_exact)) / np.float32(math.log(REL_MAX_DIST / max_exact))
                         * np.float32(REL_BUCKETS - max_exact)).astype(np.int32)
    large = np.minimum(large, REL_BUCKETS - 1)
    return np.where(n < max_exact, n, large).astype(np.int32)


def _bias_table_kernel(rel_ref, bsub_ref, bdiag_ref, bdec_ref, tab_ref, dec_ref, new_ref):
    h = pl.program_id(0)
    far = rel_ref[REL_BUCKETS - 1, h]

    def gather(bucket):
        out = jnp.zeros(bucket.shape, F32)
        for b in range(REL_BUCKETS):
            out = jnp.where(bucket == b, rel_ref[b, h], out)
        return (out - far) * LOG2E

    tab_ref[0, 0] = gather(bsub_ref[...])
    tab_ref[0, 1] = gather(bdiag_ref[...])
    dec_ref[0] = gather(bdec_ref[...])
    new_ref[0] = jnp.zeros(new_ref.shape[1:], F32) + (rel_ref[0, h] - far) * LOG2E


def _bias_tables(rel_bias, t):
    assert t >= REL_MAX_DIST and PAGE_SIZE >= REL_MAX_DIST
    n_heads = rel_bias.shape[1]
    r = np.arange(t)[:, None]
    c = np.arange(t)[None, :]
    b_sub = _t5_bucket_np(r - c + t)
    b_diag = _t5_bucket_np(r - c)
    b_dec = _t5_bucket_np(PAGE_SIZE - np.arange(PAGE_SIZE))[None, :]
    full = lambda shape: pl.BlockSpec(shape, lambda h: (0,) * len(shape))
    return pl.pallas_call(
        _bias_table_kernel,
        grid=(n_heads,),
        in_specs=[pl.BlockSpec(memory_space=pltpu.SMEM), full((t, t)), full((t, t)), full((1, PAGE_SIZE))],
        out_specs=[pl.BlockSpec((1, 2, t, t), lambda h: (h, 0, 0, 0)),
                   pl.BlockSpec((1, 1, PAGE_SIZE), lambda h: (h, 0, 0)),
                   pl.BlockSpec((1, 1, LANES), lambda h: (h, 0, 0))],
        out_shape=[jax.ShapeDtypeStruct((n_heads, 2, t, t), F32),
                   jax.ShapeDtypeStruct((n_heads, 1, PAGE_SIZE), F32),
                   jax.ShapeDtypeStruct((n_heads, 1, LANES), F32)],
        compiler_params=_params("arbitrary"),
        name="t5_bias_tables",
    )(rel_bias, jnp.asarray(b_sub), jnp.asarray(b_diag), jnp.asarray(b_dec))


def _df_lambda(lv_ref, lam_init):
    lv = lv_ref[...]
    a = jnp.sum(lv[0:1, :] * lv[1:2, :], axis=1, keepdims=True)
    b = jnp.sum(lv[2:3, :] * lv[3:4, :], axis=1, keepdims=True)
    return jnp.exp(a) - jnp.exp(b) + lam_init


def _df_prompt_kernel(lv_ref, g_ref, q_ref, kt_ref, v_ref, tab_ref, o_ref,
                      qs_ref, s_ref, rmax_ref, m_ref, l_ref, acc_ref, *, lam_init):
    t = q_ref.shape[1]
    i = pl.program_id(2)
    qs_ref[0], qs_ref[1] = _split_halves(q_ref[0])
    m_ref[...] = jnp.full_like(m_ref, NEG_BIG)
    l_ref[...] = jnp.zeros_like(l_ref)
    acc_ref[...] = jnp.zeros_like(acc_ref)

    def scores(j, slot, bias=None, mask=None):
        kt = kt_ref[0, :, pl.ds(pl.multiple_of(j * t, t), t)]
        for c in range(2):
            s = _dot(qs_ref[c], kt)
            if bias is not None:
                s = s + bias
            if mask is not None:
                s = jnp.where(mask, s, NEG_BIG)
            s_ref[slot, c] = s
            rmax_ref[slot, c] = _rep(jnp.max(s, axis=1, keepdims=True))

    def update(j, slot):
        vb = v_ref[0, pl.ds(pl.multiple_of(j * t, t), t), :]
        for c in range(2):
            m = m_ref[c]
            m_new = jnp.maximum(m, rmax_ref[slot, c])
            alpha = jnp.exp2(m - m_new)
            p = [jnp.exp2(sb - m_new) for sb in _lane_blocks(s_ref[slot, c])]
            psum = p[0]
            for pb in p[1:]:
                psum = psum + pb
            l_ref[c] = alpha * l_ref[c] + _rep(jnp.sum(psum, axis=1, keepdims=True))
            pv = _dot(jnp.concatenate([pb.astype(BF16) for pb in p], axis=1), vb)
            acc_ref[c] = alpha * acc_ref[c] + pv
            m_ref[c] = m_new

    r = lax.broadcasted_iota(jnp.int32, (t, t), 0)
    c = lax.broadcasted_iota(jnp.int32, (t, t), 1)
    scores(i, 0, tab_ref[0, 1], c <= r)

    @pl.when(i > 0)
    def _():
        scores(i - 1, 1, tab_ref[0, 0])
        update(i, 0)

    def body(s, _):
        j = i - 2 - 2 * s
        scores(j, 0)
        update(j + 1, 1)
        scores(j - 1, 1)
        update(j, 0)
        return 0

    n_far = jnp.maximum(i - 1, 0)
    lax.fori_loop(0, n_far // 2, body, 0)

    @pl.when(n_far % 2 == 1)
    def _():
        scores(0, 0)
        update(1, 1)
        update(0, 0)

    @pl.when((n_far % 2 == 0) & (i > 0))
    def _():
        update(0, 1)

    @pl.when(i == 0)
    def _():
        update(0, 0)

    lam = _df_lambda(lv_ref, lam_init)
    o = acc_ref[0] / l_ref[0] - lam * (acc_ref[1] / l_ref[1])
    o = o * lax.rsqrt(jnp.mean(o * o, axis=1, keepdims=True) + 1e-5)
    o_ref[0] = (o * g_ref[...] * (1.0 - lam_init)).astype(o_ref.dtype)


def _df_prompt_attention(q, kt, v, tab, lam_vec, subln_g, batch, seq, lam_init):
    d = q.shape[1]
    t = tab.shape[2]
    q3, v3 = (a.reshape(batch, seq, d) for a in (q, v))
    kt_spec = pl.BlockSpec((1, LANES, seq), lambda b, h, i: (b, h, 0))
    v_spec = pl.BlockSpec((1, seq, LANES), lambda b, h, i: (b, 0, h))
    q_spec = pl.BlockSpec((1, t, LANES), lambda b, h, i: (b, i, h))
    out = pl.pallas_call(
        functools.partial(_df_prompt_kernel, lam_init=lam_init),
        grid=(batch, d // LANES, seq // t),
        in_specs=[pl.BlockSpec(lam_vec.shape, lambda b, h, i: (0, 0)),
                  pl.BlockSpec((1, LANES), lambda b, h, i: (0, 0)),
                  q_spec, kt_spec, v_spec,
                  pl.BlockSpec((1, 2, t, t), lambda b, h, i: (h, 0, 0, 0))],
        out_specs=q_spec,
        out_shape=jax.ShapeDtypeStruct((batch, seq, d), BF16),
        scratch_shapes=[pltpu.VMEM((2, t, LANES), BF16),
                        pltpu.VMEM((2, 2, t, t), F32),
                        pltpu.VMEM((2, 2, t, LANES), F32),
                        pltpu.VMEM((2, t, LANES), F32),
                        pltpu.VMEM((2, t, LANES), F32),
                        pltpu.VMEM((2, t, LANES), F32)],
        compiler_params=_params("parallel", "parallel", "arbitrary"),
        name="df_prompt",
    )(lam_vec, subln_g.reshape(1, LANES), q3, kt, v3, tab)
    return out.reshape(batch * seq, d)


def _head_rows(q_row, col_of_row):
    d = q_row.shape[1]
    q = jnp.broadcast_to(q_row.astype(F32), (DEC_HEADS, d))
    row = lax.broadcasted_iota(jnp.int32, (DEC_HEADS, d), 0)
    col = lax.broadcasted_iota(jnp.int32, (DEC_HEADS, d), 1)
    start = col_of_row(row)
    return jnp.where((col >= start) & (col < start + HEAD_DIM), q, 0.0).astype(BF16)


def _sb_decode_kernel(pt_ref, q_ref, *refs, first_page, n_pages):
    kt_hbm, vt_hbm, o_ref, kbuf, vbuf, sem, acc_ref = refs
    b = pl.program_id(0)
    n_ahead = min(2, n_pages)

    def page_copies(seq, p):
        page = first_page + pt_ref[seq, n_pages - 1 - p]
        slot = p % 2
        return (pltpu.make_async_copy(kt_hbm.at[page], kbuf.at[slot], sem.at[0, slot]),
                pltpu.make_async_copy(vt_hbm.at[page], vbuf.at[slot], sem.at[1, slot]))

    def request(seq, p):
        for cp in page_copies(seq, p):
            cp.start()

    def land(seq, p):
        for cp in page_copies(seq, p):
            cp.wait()

    @pl.when(b == 0)
    def _():
        for p in range(n_ahead):
            request(b, p)

    acc_ref[...] = jnp.zeros_like(acc_ref)
    qrows = _head_rows(q_ref[0], lambda r: r * HEAD_DIM)
    u = _suffix_matrix(PAGE_SIZE)

    def body(state):
        p, requested, wanted, cc, _ = state
        for k in range(2):
            @pl.when(requested + k < wanted)
            def _():
                request(b, requested + k)

        land(b, p)
        slot = p % 2
        z = _dot(qrows, kbuf[slot].astype(BF16))
        nls = _softplus2(z)
        after = _dot(nls.astype(BF16), u)
        a = jnp.exp2(z - nls + after + cc)
        for h in range(DEC_HEADS):
            rows = slice(h * HEAD_DIM, (h + 1) * HEAD_DIM)
            acc_ref[rows, :] += vbuf[slot, rows, :] * a[h:h + 1, :]
        cc = cc + after[:, 0:1] - _bf16_round(nls[:, 0:1])
        alive = jnp.max(cc) > F32_EXP2_ZERO
        ahead = jnp.where(alive & (p >= 1), jnp.minimum(p + 3, n_pages), wanted)
        return p + 1, wanted, ahead, cc, alive

    start = (jnp.int32(0), jnp.int32(n_ahead), jnp.int32(n_ahead), jnp.zeros((DEC_HEADS, 1), F32), True)
    p_end, requested, _, _, _ = lax.while_loop(lambda state: (state[0] < n_pages) & state[4], body, start)

    for k in range(2):
        @pl.when(p_end + k < requested)
        def _():
            land(b, p_end + k)

    @pl.when(b + 1 < pl.num_programs(0))
    def _():
        for p in range(n_ahead):
            request(b + 1, p)

    o_ref[0] = jnp.sum(acc_ref[...].T, axis=0, keepdims=True).astype(o_ref.dtype)


def _page_specs(n_pages, npg, block, first_page):
    return [pl.BlockSpec((1,) + block, lambda b, s, pt, p=p: (first_page + pt[b, s * npg + p], 0, 0))
            for p in range(npg)]


def _sb_decode_attention(q, cache_kt, cache_vt, page_table, first_page):
    db, d = q.shape
    row = pl.BlockSpec((1, 1, d), lambda b, pt: (b, 0, 0))
    in_hbm = pl.BlockSpec(memory_space=pl.ANY)
    out = pl.pallas_call(
        functools.partial(_sb_decode_kernel, first_page=first_page, n_pages=page_table.shape[1]),
        grid_spec=pltpu.PrefetchScalarGridSpec(
            num_scalar_prefetch=1,
            grid=(db,),
            in_specs=[row, in_hbm, in_hbm],
            out_specs=row,
            scratch_shapes=[pltpu.VMEM((2, d, PAGE_SIZE), F32),
                            pltpu.VMEM((2, d, PAGE_SIZE), F32),
                            pltpu.SemaphoreType.DMA((2, 2)),
                            pltpu.VMEM((d, PAGE_SIZE), F32)]),
        out_shape=jax.ShapeDtypeStruct((db, 1, d), BF16),
        compiler_params=_params("arbitrary"),
        name="sb_decode",
    )(page_table, q.reshape(db, 1, d), cache_kt, cache_vt)
    return out.reshape(db, d)


def _df_decode_kernel(pt_ref, lv_ref, g_ref, dec_ref, new_ref, spread_ref, q_ref, kn_ref, vn_ref, *refs, lam_init):
    del pt_ref
    npg = (len(refs) - 4) // 2
    kt_refs, v_refs = refs[:npg], refs[npg:2 * npg]
    o_ref, acc_ref, m_ref, l_ref = refs[2 * npg:]
    s = pl.program_id(1)
    last = s == pl.num_programs(1) - 1
    n_heads = DEC_HEADS // 2

    @pl.when(s == 0)
    def _():
        acc_ref[...] = jnp.zeros_like(acc_ref)
        m_ref[...] = jnp.full_like(m_ref, NEG_BIG)
        l_ref[...] = jnp.zeros_like(l_ref)

    qrows = _head_rows(q_ref[0], lambda r: (r % n_heads) * LANES + (r // n_heads) * HEAD_DIM)
    scores = []
    for p in range(npg):
        sp = _dot(qrows, kt_refs[p][0].astype(BF16))
        if p == npg - 1:
            sp = sp + jnp.where(last, dec_ref[...], 0.0)
        scores.append(sp)
    m_old = m_ref[...]
    m_new = m_old
    for sp in scores:
        m_new = jnp.maximum(m_new, jnp.max(sp, axis=1, keepdims=True))
    alpha = jnp.exp2(m_old - m_new)
    l = alpha * l_ref[...]
    acc = alpha * acc_ref[...]
    row = lax.broadcasted_iota(jnp.int32, (DEC_HEADS, PAGE_SIZE * n_heads), 0)
    lane = lax.broadcasted_iota(jnp.int32, (DEC_HEADS, PAGE_SIZE * n_heads), 1)
    own_head = lane % n_heads == row % n_heads
    for p, sp in enumerate(scores):
        pr = jnp.exp2(sp - m_new)
        l = l + jnp.sum(pr, axis=1, keepdims=True)
        spread = _dot(pr.astype(BF16), spread_ref[...])
        spread = jnp.where(own_head, spread, 0.0).astype(BF16)
        acc = acc + _dot(spread, v_refs[p][0].astype(BF16))
    m_ref[...] = m_new
    l_ref[...] = l
    acc_ref[...] = acc

    @pl.when(last)
    def _():
        kn = kn_ref[0].astype(F32)
        s_new = jnp.sum(qrows.astype(F32) * kn, axis=1, keepdims=True) + new_ref[...][:, 0:1]
        m_fin = jnp.maximum(m_new, s_new)
        a_old = jnp.exp2(m_new - m_fin)
        p_new = jnp.exp2(s_new - m_fin)
        l_fin = a_old * l + p_new
        vn = vn_ref[0]
        acc_fin = a_old * acc + p_new * jnp.concatenate([vn, vn], axis=0)
        norm = acc_fin / l_fin
        lam = _df_lambda(lv_ref, lam_init)
        o = norm[0:n_heads] - lam * norm[n_heads:DEC_HEADS]
        o = o * lax.rsqrt(jnp.mean(o * o, axis=1, keepdims=True) + 1e-5)
        o_ref[0] = (o * g_ref[...] * (1.0 - lam_init)).astype(o_ref.dtype)


def _df_decode_attention(q, k_new, v_new, cache_kt, cache_v, page_table, first_page, dec_bias, new_bias,
                         lam_vec, subln_g, lam_init):
    db, d = q.shape
    n_pages = page_table.shape[1]
    npg = min(DEC_PAGES_PER_STEP, n_pages)
    n_heads = d // LANES
    row = pl.BlockSpec((1, 1, d), lambda b, s, pt: (b, 0, 0))
    head_rows = pl.BlockSpec((1, n_heads, LANES), lambda b, s, pt: (b, 0, 0))
    const = lambda shape: pl.BlockSpec(shape, lambda b, s, pt: (0,) * len(shape))
    kt_pages = _page_specs(n_pages, npg, (d, PAGE_SIZE), first_page)
    v_pages = _page_specs(n_pages, npg, (PAGE_SIZE * n_heads, LANES), first_page)
    spread = np.repeat(np.eye(PAGE_SIZE), n_heads, axis=1)
    out = pl.pallas_call(
        functools.partial(_df_decode_kernel, lam_init=lam_init),
        grid_spec=pltpu.PrefetchScalarGridSpec(
            num_scalar_prefetch=1,
            grid=(db, n_pages // npg),
            in_specs=[const(lam_vec.shape), const((1, LANES)), const((DEC_HEADS, PAGE_SIZE)),
                      const((DEC_HEADS, LANES)), const(spread.shape), row, row, head_rows] + kt_pages + v_pages,
            out_specs=head_rows,
            scratch_shapes=[pltpu.VMEM((DEC_HEADS, LANES), F32), pltpu.VMEM((DEC_HEADS, 1), F32),
                            pltpu.VMEM((DEC_HEADS, 1), F32)]),
        out_shape=jax.ShapeDtypeStruct((db, n_heads, LANES), BF16),
        compiler_params=_params("parallel", "arbitrary"),
        name="df_decode",
    )(page_table, lam_vec, subln_g.reshape(1, LANES), dec_bias, new_bias, jnp.asarray(spread, BF16),
      q.reshape(db, 1, d), k_new.reshape(db, 1, d), v_new.reshape(db, n_heads, LANES),
      *([cache_kt] * npg), *([cache_v] * npg))
    return out.reshape(db, d)


def kernel(x_prompt, x_sample, cache_sb_k, cache_sb_v, cache_df_k, cache_df_v, page_table,
           w_in, w_out, ln_g, ln_b, df_lambda, df_subln_g, rel_bias, w_up, w_down):
    batch, seq, d = x_prompt.shape
    db = x_sample.shape[0]
    depth = w_in.shape[0]
    n_sb_heads = d // HEAD_DIM
    n_df_heads = d // (2 * HEAD_DIM)
    alpha = (2 * depth) ** 0.25
    q_scale = ATT_SCALE * LOG2E
    t = min(ATT_BLOCK, seq)
    assert x_sample.shape[1] == 1 and d // HEAD_DIM == DEC_HEADS
    assert cache_sb_k.shape[2] == PAGE_SIZE and seq % t == 0 and t % MXU_DIM == 0

    xp = x_prompt.reshape(batch * seq, d)
    xs = x_sample.reshape(db, d)
    w_in_b, w_out_b = w_in.astype(BF16), w_out.astype(BF16)
    w_up_b, w_down_b = w_up.astype(BF16), w_down.astype(BF16)
    n_pool = cache_sb_k.shape[1]
    feature_major = lambda cache: jnp.moveaxis(cache, 2, -1).reshape(-1, d, PAGE_SIZE)
    df_v_pages = cache_df_v.reshape(-1, PAGE_SIZE * n_df_heads, 2 * HEAD_DIM)
    token_major = lambda xt, feat: jnp.moveaxis(xt.reshape((batch,) + feat + (seq,)), -1, 1)

    sb_kp, sb_vp, sb_ks, sb_vs = [], [], [], []
    df_kp, df_vp, df_ks, df_vs = [], [], [], []
    for i in range(depth):
        j = i // 2
        qp, kpt, kptb, vp, vpb = _qkv_proj_fm(xp, w_in_b[i], q_scale, batch, v_feature_major=(i % 2 == 0))
        qs, ks_, vs_ = _qkv_proj(xs, w_in_b[i], q_scale)
        if i % 2 == 0:
            op = _sb_prompt_attention(qp, kptb, vpb, batch, seq)
            os_ = _sb_decode_attention(qs, feature_major(cache_sb_k), feature_major(cache_sb_v),
                                       page_table, j * n_pool)
            sb_kp.append(token_major(kpt, (n_sb_heads, HEAD_DIM)))
            sb_vp.append(token_major(vp, (n_sb_heads, HEAD_DIM)))
            sb_ks.append(ks_.reshape(db, 1, n_sb_heads, HEAD_DIM))
            sb_vs.append(vs_.reshape(db, 1, n_sb_heads, HEAD_DIM))
        else:
            lam_init = 0.8 - 0.6 * math.exp(-0.3 * i)
            tab, dec_bias, new_bias = _bias_tables(rel_bias, t)
            dec_bias = jnp.tile(dec_bias.reshape(n_df_heads, PAGE_SIZE), (2, 1))
            new_bias = jnp.tile(new_bias.reshape(n_df_heads, LANES), (2, 1))
            op = _df_prompt_attention(qp, kptb, vpb, tab, df_lambda[j], df_subln_g[j], batch, seq, lam_init)
            os_ = _df_decode_attention(qs, ks_, vs_, feature_major(cache_df_k), df_v_pages, page_table, j * n_pool,
                                       dec_bias, new_bias, df_lambda[j], df_subln_g[j], lam_init)
            df_kp.append(token_major(kpt, (n_df_heads, 2, HEAD_DIM)))
            df_vp.append(vp.reshape(batch, seq, n_df_heads, 2 * HEAD_DIM))
            df_ks.append(ks_.reshape(db, 1, n_df_heads, 2, HEAD_DIM))
            df_vs.append(vs_.reshape(db, 1, n_df_heads, 2 * HEAD_DIM))
        xp = _outproj_ln(op, xp, w_out_b[i], ln_g[i, 0], ln_b[i, 0], alpha)
        xs = _outproj_ln(os_, xs, w_out_b[i], ln_g[i, 0], ln_b[i, 0], alpha)
        xp = _mlp_ln(xp, w_up_b[i], w_down_b[i], ln_g[i, 1], ln_b[i, 1], alpha)
        xs = _mlp_ln(xs, w_up_b[i], w_down_b[i], ln_g[i, 1], ln_b[i, 1], alpha)
    return (xp.reshape(batch, seq, d), xs.reshape(db, 1, d),
            jnp.stack(sb_kp), jnp.stack(sb_vp), jnp.stack(sb_ks), jnp.stack(sb_vs),
            jnp.stack(df_kp), jnp.stack(df_vp), jnp.stack(df_ks), jnp.stack(df_vs))
```

```python
import functools
import math

import numpy as np
import jax
import jax.numpy as jnp
from jax import lax
from jax.experimental import pallas as pl
from jax.experimental.pallas import tpu as pltpu

HEAD_DIM = 64
PAGE_SIZE = 128
REL_BUCKETS = 32
REL_MAX_DIST = 128
ATT_SCALE = HEAD_DIM ** -0.5
NEG_BIG = -1e30
LN_EPS = 1e-5
LOG2E = math.log2(math.e)
F32_EXP2_ZERO = -150.0

LANES = 128
MXU_DIM = 256
VMEM_LIMIT_BYTES = 48 * 1024 * 1024

ATT_BLOCK = 2 * MXU_DIM
SB_BLOCK = MXU_DIM
ROW_BLOCK = 512
FF_CHUNK = 4096
DEC_HEADS = 16
DEC_PAGES_PER_STEP = 16

F32 = jnp.float32
BF16 = jnp.bfloat16


def _params(*semantics):
    return pltpu.CompilerParams(dimension_semantics=semantics,
                                vmem_limit_bytes=VMEM_LIMIT_BYTES)


def _dot(a, b):
    return jnp.dot(a, b, preferred_element_type=F32)


def _dot_nt(a, b):
    return lax.dot_general(a, b, (((1,), (1,)), ((), ())), preferred_element_type=F32)


def _suffix_matrix_np(n):
    return np.where(np.arange(n)[:, None] > np.arange(n)[None, :], -1.0, 0.0)


def _suffix_matrix(n):
    r = lax.broadcasted_iota(jnp.int32, (n, n), 0)
    c = lax.broadcasted_iota(jnp.int32, (n, n), 1)
    return jnp.where(r > c, -1.0, 0.0).astype(BF16)


def _split_halves(q2):
    qf = q2.astype(F32)
    first = lax.broadcasted_iota(jnp.int32, qf.shape, 1) < HEAD_DIM
    return jnp.where(first, qf, 0.0).astype(BF16), jnp.where(first, 0.0, qf).astype(BF16)


def _bf16_round(x):
    return x.astype(BF16).astype(F32)


def _rep(col):
    return jnp.broadcast_to(col, (col.shape[0], LANES))


def _lane_blocks(x):
    return [x[:, cb * LANES:(cb + 1) * LANES] for cb in range(x.shape[1] // LANES)]


def _softplus2(z):
    neg_abs = pltpu.bitcast(pltpu.bitcast(z, jnp.uint32) | jnp.uint32(0x80000000), F32)
    return jnp.maximum(z, 0.0) + jnp.log2(1.0 + jnp.exp2(neg_abs))


def _proj_kernel(x_ref, w_ref, q_ref, k_ref, v_ref, *, q_scale):
    d = x_ref.shape[1]
    x = x_ref[...].astype(BF16)
    q_ref[...] = (_dot(x, w_ref[:, 0:d]) * q_scale).astype(BF16)
    k_ref[...] = _dot(x, w_ref[:, d:2 * d])
    v_ref[...] = _dot(x, w_ref[:, 2 * d:3 * d])


def _qkv_proj(x, w_bf16, q_scale):
    m, d = x.shape
    bm = min(ROW_BLOCK, m)
    row = pl.BlockSpec((bm, d), lambda i: (i, 0))
    return pl.pallas_call(
        functools.partial(_proj_kernel, q_scale=q_scale),
        grid=(m // bm,),
        in_specs=[row, pl.BlockSpec((d, 3 * d), lambda i: (0, 0))],
        out_specs=[row, row, row],
        out_shape=[jax.ShapeDtypeStruct((m, d), BF16),
                   jax.ShapeDtypeStruct((m, d), F32),
                   jax.ShapeDtypeStruct((m, d), F32)],
        compiler_params=_params("parallel"),
        name="qkv_proj",
    )(x, w_bf16)


def _proj_fm_kernel(x_ref, w_ref, q_ref, kt_ref, ktb_ref, v_ref, vb_ref, *, q_scale, v_feature_major):
    d = x_ref.shape[1]
    x = x_ref[...].astype(BF16)
    q_ref[...] = (_dot(x, w_ref[:, 0:d]) * q_scale).astype(BF16)
    kt = _dot(x, w_ref[:, d:2 * d]).T
    kt_ref[0] = kt
    ktb_ref[0] = kt.astype(BF16)
    v = _dot(x, w_ref[:, 2 * d:3 * d])
    vb_ref[...] = v.astype(BF16)
    if v_feature_major:
        v_ref[0] = v.T
    else:
        v_ref[...] = v


def _qkv_proj_fm(x, w_bf16, q_scale, batch, v_feature_major):
    m, d = x.shape
    seq = m // batch
    bm = min(ROW_BLOCK, seq)
    nb = seq // bm
    row = pl.BlockSpec((bm, d), lambda i: (i, 0))
    fm = pl.BlockSpec((1, d, bm), lambda i: (i // nb, 0, i % nb))
    fm_f32 = jax.ShapeDtypeStruct((batch, d, seq), F32)
    return pl.pallas_call(
        functools.partial(_proj_fm_kernel, q_scale=q_scale, v_feature_major=v_feature_major),
        grid=(m // bm,),
        in_specs=[row, pl.BlockSpec((d, 3 * d), lambda i: (0, 0))],
        out_specs=[row, fm, fm, fm if v_feature_major else row, row],
        out_shape=[jax.ShapeDtypeStruct((m, d), BF16),
                   fm_f32,
                   jax.ShapeDtypeStruct((batch, d, seq), BF16),
                   fm_f32 if v_feature_major else jax.ShapeDtypeStruct((m, d), F32),
                   jax.ShapeDtypeStruct((m, d), BF16)],
        compiler_params=_params("parallel"),
        name="qkv_proj_fm",
    )(x, w_bf16)


def _layernorm(xf, g, b):
    mu = jnp.mean(xf, axis=-1, keepdims=True)
    xc = xf - mu
    var = jnp.mean(xc * xc, axis=-1, keepdims=True)
    return xc * lax.rsqrt(var + LN_EPS) * g + b


def _outproj_ln_kernel(o_ref, x_ref, w_ref, g_ref, b_ref, y_ref, *, alpha):
    m = _dot(o_ref[...], w_ref[...])
    y_ref[...] = _layernorm(alpha * x_ref[...] + m, g_ref[...], b_ref[...])


def _outproj_ln(o_bf16, x, w_bf16, g, b, alpha):
    m, d = x.shape
    bm = min(ROW_BLOCK, m)
    row = pl.BlockSpec((bm, d), lambda i: (i, 0))
    vec = pl.BlockSpec((1, d), lambda i: (0, 0))
    return pl.pallas_call(
        functools.partial(_outproj_ln_kernel, alpha=alpha),
        grid=(m // bm,),
        in_specs=[row, row, pl.BlockSpec((d, d), lambda i: (0, 0)), vec, vec],
        out_specs=row,
        out_shape=jax.ShapeDtypeStruct((m, d), F32),
        compiler_params=_params("parallel"),
        name="outproj_ln",
    )(o_bf16, x, w_bf16, g.reshape(1, d), b.reshape(1, d))


def _mlp_ln_kernel(x_ref, wu_ref, wd_ref, g_ref, b_ref, y_ref, acc_ref, *, alpha):
    c = pl.program_id(1)
    x = x_ref[...]
    h = jnp.maximum(_dot(x.astype(BF16), wu_ref[...]), 0.0)
    part = _dot((h * h).astype(BF16), wd_ref[...])

    @pl.when(c == 0)
    def _():
        acc_ref[...] = part

    @pl.when(c > 0)
    def _():
        acc_ref[...] += part

    @pl.when(c == pl.num_programs(1) - 1)
    def _():
        y_ref[...] = _layernorm(alpha * x + acc_ref[...], g_ref[...], b_ref[...])


def _mlp_ln(x, wu_bf16, wd_bf16, g, b, alpha):
    m, d = x.shape
    dff = wu_bf16.shape[1]
    bm = min(ROW_BLOCK, m)
    fc = min(FF_CHUNK, dff)
    row = pl.BlockSpec((bm, d), lambda i, c: (i, 0))
    vec = pl.BlockSpec((1, d), lambda i, c: (0, 0))
    return pl.pallas_call(
        functools.partial(_mlp_ln_kernel, alpha=alpha),
        grid=(m // bm, dff // fc),
        in_specs=[row,
                  pl.BlockSpec((d, fc), lambda i, c: (0, c)),
                  pl.BlockSpec((fc, d), lambda i, c: (c, 0)),
                  vec, vec],
        out_specs=row,
        out_shape=jax.ShapeDtypeStruct((m, d), F32),
        scratch_shapes=[pltpu.VMEM((bm, d), F32)],
        compiler_params=_params("parallel", "arbitrary"),
        name="mlp_ln",
    )(x, wu_bf16, wd_bf16, g.reshape(1, d), b.reshape(1, d))


def _sb_scores(qh, kt, mask, nls_ref, nls0_ref, logb_ref):
    z = _dot(qh, kt)
    nls = _softplus2(z)
    logb = z - nls
    if mask is not None:
        nls = jnp.where(mask, nls, 0.0)
        logb = jnp.where(mask, logb, NEG_BIG)
    nls_ref[...] = nls.astype(BF16)
    for ch in range(nls.shape[1] // MXU_DIM):
        nls0_ref[ch] = _rep(_bf16_round(nls[:, ch * MXU_DIM:ch * MXU_DIM + 1]))
    logb_ref[...] = logb


def _sb_weights(nls_ref, nls0_ref, logb_ref, vb, u, acc_ref, c_ref):
    t = nls_ref.shape[0]
    n_chunks = nls_ref.shape[1] // MXU_DIM
    later = jnp.zeros((t, LANES), F32)
    a = [None] * n_chunks
    for ch in reversed(range(n_chunks)):
        cols = slice(ch * MXU_DIM, (ch + 1) * MXU_DIM)
        after = _dot(nls_ref[:, cols], u)
        blocks = []
        for lb in range(MXU_DIM // LANES):
            lanes = slice(lb * LANES, (lb + 1) * LANES)
            e = logb_ref[:, cols][:, lanes] + after[:, lanes]
            blocks.append(jnp.exp2(e if ch == n_chunks - 1 else e + later).astype(BF16))
        a[ch] = jnp.concatenate(blocks, axis=1)
        later = later + (_rep(after[:, 0:1]) - nls0_ref[ch])
    o = _dot(jnp.concatenate(a, axis=1), vb)
    c = c_ref[...]
    acc_ref[...] += jnp.exp2(c) * o
    c_ref[...] = c + later


def _sb_prompt_kernel(u_ref, q_ref, kt_ref, v_ref, o_ref, qs_ref, nls_ref, nls0_ref, logb_ref, acc_ref, c_ref):
    t = q_ref.shape[1]
    i = pl.program_id(2)
    qs_ref[0], qs_ref[1] = _split_halves(q_ref[0])
    acc_ref[...] = jnp.zeros_like(acc_ref)
    c_ref[...] = jnp.zeros_like(c_ref)

    def scores(j, slot, mask=None):
        kt = kt_ref[0, :, pl.ds(pl.multiple_of(j * t, t), t)]
        for h in range(2):
            _sb_scores(qs_ref[h], kt, mask, nls_ref.at[slot, h], nls0_ref.at[slot, h], logb_ref.at[slot, h])

    def weights(j, slot):
        vb = v_ref[0, pl.ds(pl.multiple_of(j * t, t), t), :]
        for h in range(2):
            _sb_weights(nls_ref.at[slot, h], nls0_ref.at[slot, h], logb_ref.at[slot, h], vb, u_ref[...],
                        acc_ref.at[h], c_ref.at[h])

    r = lax.broadcasted_iota(jnp.int32, (t, t), 0)
    c = lax.broadcasted_iota(jnp.int32, (t, t), 1)
    scores(i, 0, c < r)

    def stick_left():
        return jnp.max(c_ref[...]) > F32_EXP2_ZERO

    def body(state):
        s, _ = state
        j = i - 1 - 2 * s
        scores(j, 1)
        weights(j + 1, 0)
        scores(j - 1, 0)
        weights(j, 1)
        return s + 1, stick_left()

    _, alive = lax.while_loop(lambda state: (state[0] < i // 2) & state[1], body, (jnp.int32(0), True))

    @pl.when(alive & (i % 2 == 1))
    def _():
        scores(0, 1)
        weights(1, 0)
        weights(0, 1)

    @pl.when(alive & (i % 2 == 0))
    def _():
        weights(0, 0)

    lane_o = lax.broadcasted_iota(jnp.int32, (t, LANES), 1)
    o_ref[0] = jnp.where(lane_o < HEAD_DIM, acc_ref[0], acc_ref[1]).astype(o_ref.dtype)


def _sb_prompt_attention(q, kt, v, batch, seq):
    d = q.shape[1]
    t = min(SB_BLOCK, seq)
    assert seq % t == 0 and t % MXU_DIM == 0
    q3, v3 = (a.reshape(batch, seq, d) for a in (q, v))
    kt_spec = pl.BlockSpec((1, LANES, seq), lambda b, h, i: (b, h, 0))
    v_spec = pl.BlockSpec((1, seq, LANES), lambda b, h, i: (b, 0, h))
    q_spec = pl.BlockSpec((1, t, LANES), lambda b, h, i: (b, i, h))
    out = pl.pallas_call(
        _sb_prompt_kernel,
        grid=(batch, d // LANES, seq // t),
        in_specs=[pl.BlockSpec((MXU_DIM, MXU_DIM), lambda b, h, i: (0, 0)), q_spec, kt_spec, v_spec],
        out_specs=q_spec,
        out_shape=jax.ShapeDtypeStruct((batch, seq, d), BF16),
        scratch_shapes=[pltpu.VMEM((2, t, LANES), BF16),
                        pltpu.VMEM((2, 2, t, t), BF16),
                        pltpu.VMEM((2, 2, t // MXU_DIM, t, LANES), F32),
                        pltpu.VMEM((2, 2, t, t), F32),
                        pltpu.VMEM((2, t, LANES), F32),
                        pltpu.VMEM((2, t, LANES), F32)],
        compiler_params=_params("parallel", "parallel", "arbitrary"),
        name="sb_prompt",
    )(jnp.asarray(_suffix_matrix_np(MXU_DIM), BF16), q3, kt, v3)
    return out.reshape(batch * seq, d)


def _t5_bucket_np(dist):
    n = np.maximum(dist, 0).astype(np.int32)
    max_exact = REL_BUCKETS // 2
    nf = np.maximum(n, 1).astype(np.float32)
    large = max_exact + (np.log(nf / np.float32(max_exact)) / np.float32(math.log(REL_MAX_DIST / max_exact))
                         * np.float32(REL_BUCKETS - max_exact)).astype(np.int32)
    large = np.minimum(large, REL_BUCKETS - 1)
    return np.where(n < max_exact, n, large).astype(np.int32)


def _bias_table_kernel(rel_ref, bsub_ref, bdiag_ref, bdec_ref, tab_ref, dec_ref, new_ref):
    h = pl.program_id(0)
    far = rel_ref[REL_BUCKETS - 1, h]

    def gather(bucket):
        out = jnp.zeros(bucket.shape, F32)
        for b in range(REL_BUCKETS):
            out = jnp.where(bucket == b, rel_ref[b, h], out)
        return (out - far) * LOG2E

    tab_ref[0, 0] = gather(bsub_ref[...])
    tab_ref[0, 1] = gather(bdiag_ref[...])
    dec_ref[0] = gather(bdec_ref[...])
    new_ref[0] = jnp.zeros(new_ref.shape[1:], F32) + (rel_ref[0, h] - far) * LOG2E


def _bias_tables(rel_bias, t):
    assert t >= REL_MAX_DIST and PAGE_SIZE >= REL_MAX_DIST
    n_heads = rel_bias.shape[1]
    r = np.arange(t)[:, None]
    c = np.arange(t)[None, :]
    b_sub = _t5_bucket_np(r - c + t)
    b_diag = _t5_bucket_np(r - c)
    b_dec = _t5_bucket_np(PAGE_SIZE - np.arange(PAGE_SIZE))[None, :]
    full = lambda shape: pl.BlockSpec(shape, lambda h: (0,) * len(shape))
    return pl.pallas_call(
        _bias_table_kernel,
        grid=(n_heads,),
        in_specs=[pl.BlockSpec(memory_space=pltpu.SMEM), full((t, t)), full((t, t)), full((1, PAGE_SIZE))],
        out_specs=[pl.BlockSpec((1, 2, t, t), lambda h: (h, 0, 0, 0)),
                   pl.BlockSpec((1, 1, PAGE_SIZE), lambda h: (h, 0, 0)),
                   pl.BlockSpec((1, 1, LANES), lambda h: (h, 0, 0))],
        out_shape=[jax.ShapeDtypeStruct((n_heads, 2, t, t), F32),
                   jax.ShapeDtypeStruct((n_heads, 1, PAGE_SIZE), F32),
                   jax.ShapeDtypeStruct((n_heads, 1, LANES), F32)],
        compiler_params=_params("arbitrary"),
        name="t5_bias_tables",
    )(rel_bias, jnp.asarray(b_sub), jnp.asarray(b_diag), jnp.asarray(b_dec))


def _df_lambda(lv_ref, lam_init):
    lv = lv_ref[...]
    a = jnp.sum(lv[0:1, :] * lv[1:2, :], axis=1, keepdims=True)
    b = jnp.sum(lv[2:3, :] * lv[3:4, :], axis=1, keepdims=True)
    return jnp.exp(a) - jnp.exp(b) + lam_init


def _df_prompt_kernel(lv_ref, g_ref, q_ref, kt_ref, v_ref, tab_ref, o_ref,
                      qs_ref, s_ref, rmax_ref, m_ref, l_ref, acc_ref, *, lam_init):
    t = q_ref.shape[1]
    i = pl.program_id(2)
    qs_ref[0], qs_ref[1] = _split_halves(q_ref[0])
    m_ref[...] = jnp.full_like(m_ref, NEG_BIG)
    l_ref[...] = jnp.zeros_like(l_ref)
    acc_ref[...] = jnp.zeros_like(acc_ref)

    def scores(j, slot, bias=None, mask=None):
        kt = kt_ref[0, :, pl.ds(pl.multiple_of(j * t, t), t)]
        for c in range(2):
            s = _dot(qs_ref[c], kt)
            if bias is not None:
                s = s + bias
            if mask is not None:
                s = jnp.where(mask, s, NEG_BIG)
            s_ref[slot, c] = s
            rmax_ref[slot, c] = _rep(jnp.max(s, axis=1, keepdims=True))

    def update(j, slot):
        vb = v_ref[0, pl.ds(pl.multiple_of(j * t, t), t), :]
        for c in range(2):
            m = m_ref[c]
            m_new = jnp.maximum(m, rmax_ref[slot, c])
            alpha = jnp.exp2(m - m_new)
            p = [jnp.exp2(sb - m_new) for sb in _lane_blocks(s_ref[slot, c])]
            psum = p[0]
            for pb in p[1:]:
                psum = psum + pb
            l_ref[c] = alpha * l_ref[c] + _rep(jnp.sum(psum, axis=1, keepdims=True))
            pv = _dot(jnp.concatenate([pb.astype(BF16) for pb in p], axis=1), vb)
            acc_ref[c] = alpha * acc_ref[c] + pv
            m_ref[c] = m_new

    r = lax.broadcasted_iota(jnp.int32, (t, t), 0)
    c = lax.broadcasted_iota(jnp.int32, (t, t), 1)
    scores(i, 0, tab_ref[0, 1], c <= r)

    @pl.when(i > 0)
    def _():
        scores(i - 1, 1, tab_ref[0, 0])
        update(i, 0)

    def body(s, _):
        j = i - 2 - 2 * s
        scores(j, 0)
        update(j + 1, 1)
        scores(j - 1, 1)
        update(j, 0)
        return 0

    n_far = jnp.maximum(i - 1, 0)
    lax.fori_loop(0, n_far // 2, body, 0)

    @pl.when(n_far % 2 == 1)
    def _():
        scores(0, 0)
        update(1, 1)
        update(0, 0)

    @pl.when((n_far % 2 == 0) & (i > 0))
    def _():
        update(0, 1)

    @pl.when(i == 0)
    def _():
        update(0, 0)

    lam = _df_lambda(lv_ref, lam_init)
    o = acc_ref[0] / l_ref[0] - lam * (acc_ref[1] / l_ref[1])
    o = o * lax.rsqrt(jnp.mean(o * o, axis=1, keepdims=True) + 1e-5)
    o_ref[0] = (o * g_ref[...] * (1.0 - lam_init)).astype(o_ref.dtype)


def _df_prompt_attention(q, kt, v, tab, lam_vec, subln_g, batch, seq, lam_init):
    d = q.shape[1]
    t = tab.shape[2]
    q3, v3 = (a.reshape(batch, seq, d) for a in (q, v))
    kt_spec = pl.BlockSpec((1, LANES, seq), lambda b, h, i: (b, h, 0))
    v_spec = pl.BlockSpec((1, seq, LANES), lambda b, h, i: (b, 0, h))
    q_spec = pl.BlockSpec((1, t, LANES), lambda b, h, i: (b, i, h))
    out = pl.pallas_call(
        functools.partial(_df_prompt_kernel, lam_init=lam_init),
        grid=(batch, d // LANES, seq // t),
        in_specs=[pl.BlockSpec(lam_vec.shape, lambda b, h, i: (0, 0)),
                  pl.BlockSpec((1, LANES), lambda b, h, i: (0, 0)),
                  q_spec, kt_spec, v_spec,
                  pl.BlockSpec((1, 2, t, t), lambda b, h, i: (h, 0, 0, 0))],
        out_specs=q_spec,
        out_shape=jax.ShapeDtypeStruct((batch, seq, d), BF16),
        scratch_shapes=[pltpu.VMEM((2, t, LANES), BF16),
                        pltpu.VMEM((2, 2, t, t), F32),
                        pltpu.VMEM((2, 2, t, LANES), F32),
                        pltpu.VMEM((2, t, LANES), F32),
                        pltpu.VMEM((2, t, LANES), F32),
                        pltpu.VMEM((2, t, LANES), F32)],
        compiler_params=_params("parallel", "parallel", "arbitrary"),
        name="df_prompt",
    )(lam_vec, subln_g.reshape(1, LANES), q3, kt, v3, tab)
    return out.reshape(batch * seq, d)


def _head_rows(q_row, col_of_row):
    d = q_row.shape[1]
    q = jnp.broadcast_to(q_row.astype(F32), (DEC_HEADS, d))
    row = lax.broadcasted_iota(jnp.int32, (DEC_HEADS, d), 0)
    col = lax.broadcasted_iota(jnp.int32, (DEC_HEADS, d), 1)
    start = col_of_row(row)
    return jnp.where((col >= start) & (col < start + HEAD_DIM), q, 0.0).astype(BF16)


def _sb_decode_kernel(pt_ref, q_ref, *refs, first_page, n_pages):
    kt_hbm, vt_hbm, o_ref, kbuf, vbuf, sem, acc_ref = refs
    b = pl.program_id(0)
    n_ahead = min(2, n_pages)

    def page_copies(seq, p):
        page = first_page + pt_ref[seq, n_pages - 1 - p]
        slot = p % 2
        return (pltpu.make_async_copy(kt_hbm.at[page], kbuf.at[slot], sem.at[0, slot]),
                pltpu.make_async_copy(vt_hbm.at[page], vbuf.at[slot], sem.at[1, slot]))

    def request(seq, p):
        for cp in page_copies(seq, p):
            cp.start()

    def land(seq, p):
        for cp in page_copies(seq, p):
            cp.wait()

    @pl.when(b == 0)
    def _():
        for p in range(n_ahead):
            request(b, p)

    acc_ref[...] = jnp.zeros_like(acc_ref)
    qrows = _head_rows(q_ref[0], lambda r: r * HEAD_DIM)
    u = _suffix_matrix(PAGE_SIZE)

    def body(state):
        p, requested, wanted, cc, _ = state
        for k in range(2):
            @pl.when(requested + k < wanted)
            def _():
                request(b, requested + k)

        land(b, p)
        slot = p % 2
        z = _dot(qrows, kbuf[slot].astype(BF16))
        nls = _softplus2(z)
        after = _dot(nls.astype(BF16), u)
        a = jnp.exp2(z - nls + after + cc)
        for h in range(DEC_HEADS):
            rows = slice(h * HEAD_DIM, (h + 1) * HEAD_DIM)
            acc_ref[rows, :] += vbuf[slot, rows, :] * a[h:h + 1, :]
        cc = cc + after[:, 0:1] - _bf16_round(nls[:, 0:1])
        alive = jnp.max(cc) > F32_EXP2_ZERO
        ahead = jnp.where(alive & (p >= 1), jnp.minimum(p + 3, n_pages), wanted)
        return p + 1, wanted, ahead, cc, alive

    start = (jnp.int32(0), jnp.int32(n_ahead), jnp.int32(n_ahead), jnp.zeros((DEC_HEADS, 1), F32), True)
    p_end, requested, _, _, _ = lax.while_loop(lambda state: (state[0] < n_pages) & state[4], body, start)

    for k in range(2):
        @pl.when(p_end + k < requested)
        def _():
            land(b, p_end + k)

    @pl.when(b + 1 < pl.num_programs(0))
    def _():
        for p in range(n_ahead):
            request(b + 1, p)

    o_ref[0] = jnp.sum(acc_ref[...].T, axis=0, keepdims=True).astype(o_ref.dtype)


def _page_specs(n_pages, npg, block, first_page):
    return [pl.BlockSpec((1,) + block, lambda b, s, pt, p=p: (first_page + pt[b, s * npg + p], 0, 0))
            for p in range(npg)]


def _sb_decode_attention(q, cache_kt, cache_vt, page_table, first_page):
    db, d = q.shape
    row = pl.BlockSpec((1, 1, d), lambda b, pt: (b, 0, 0))
    in_hbm = pl.BlockSpec(memory_space=pl.ANY)
    out = pl.pallas_call(
        functools.partial(_sb_decode_kernel, first_page=first_page, n_pages=page_table.shape[1]),
        grid_spec=pltpu.PrefetchScalarGridSpec(
            num_scalar_prefetch=1,
            grid=(db,),
            in_specs=[row, in_hbm, in_hbm],
            out_specs=row,
            scratch_shapes=[pltpu.VMEM((2, d, PAGE_SIZE), F32),
                            pltpu.VMEM((2, d, PAGE_SIZE), F32),
                            pltpu.SemaphoreType.DMA((2, 2)),
                            pltpu.VMEM((d, PAGE_SIZE), F32)]),
        out_shape=jax.ShapeDtypeStruct((db, 1, d), BF16),
        compiler_params=_params("arbitrary"),
        name="sb_decode",
    )(page_table, q.reshape(db, 1, d), cache_kt, cache_vt)
    return out.reshape(db, d)


def _df_decode_kernel(pt_ref, lv_ref, g_ref, dec_ref, new_ref, spread_ref, q_ref, kn_ref, vn_ref, *refs, lam_init):
    del pt_ref
    npg = (len(refs) - 4) // 2
    kt_refs, v_refs = refs[:npg], refs[npg:2 * npg]
    o_ref, acc_ref, m_ref, l_ref = refs[2 * npg:]
    s = pl.program_id(1)
    last = s == pl.num_programs(1) - 1
    n_heads = DEC_HEADS // 2

    @pl.when(s == 0)
    def _():
        acc_ref[...] = jnp.zeros_like(acc_ref)
        m_ref[...] = jnp.full_like(m_ref, NEG_BIG)
        l_ref[...] = jnp.zeros_like(l_ref)

    qrows = _head_rows(q_ref[0], lambda r: (r % n_heads) * LANES + (r // n_heads) * HEAD_DIM)
    scores = []
    for p in range(npg):
        sp = _dot(qrows, kt_refs[p][0].astype(BF16))
        if p == npg - 1:
            sp = sp + jnp.where(last, dec_ref[...], 0.0)
        scores.append(sp)
    m_old = m_ref[...]
    m_new = m_old
    for sp in scores:
        m_new = jnp.maximum(m_new, jnp.max(sp, axis=1, keepdims=True))
    alpha = jnp.exp2(m_old - m_new)
    l = alpha * l_ref[...]
    acc = alpha * acc_ref[...]
    row = lax.broadcasted_iota(jnp.int32, (DEC_HEADS, PAGE_SIZE * n_heads), 0)
    lane = lax.broadcasted_iota(jnp.int32, (DEC_HEADS, PAGE_SIZE * n_heads), 1)
    own_head = lane % n_heads == row % n_heads
    for p, sp in enumerate(scores):
        pr = jnp.exp2(sp - m_new)
        l = l + jnp.sum(pr, axis=1, keepdims=True)
        spread = _dot(pr.astype(BF16), spread_ref[...])
        spread = jnp.where(own_head, spread, 0.0).astype(BF16)
        acc = acc + _dot(spread, v_refs[p][0].astype(BF16))
    m_ref[...] = m_new
    l_ref[...] = l
    acc_ref[...] = acc

    @pl.when(last)
    def _():
        kn = kn_ref[0].astype(F32)
        s_new = jnp.sum(qrows.astype(F32) * kn, axis=1, keepdims=True) + new_ref[...][:, 0:1]
        m_fin = jnp.maximum(m_new, s_new)
        a_old = jnp.exp2(m_new - m_fin)
        p_new = jnp.exp2(s_new - m_fin)
        l_fin = a_old * l + p_new
        vn = vn_ref[0]
        acc_fin = a_old * acc + p_new * jnp.concatenate([vn, vn], axis=0)
        norm = acc_fin / l_fin
        lam = _df_lambda(lv_ref, lam_init)
        o = norm[0:n_heads] - lam * norm[n_heads:DEC_HEADS]
        o = o * lax.rsqrt(jnp.mean(o * o, axis=1, keepdims=True) + 1e-5)
        o_ref[0] = (o * g_ref[...] * (1.0 - lam_init)).astype(o_ref.dtype)


def _df_decode_attention(q, k_new, v_new, cache_kt, cache_v, page_table, first_page, dec_bias, new_bias,
                         lam_vec, subln_g, lam_init):
    db, d = q.shape
    n_pages = page_table.shape[1]
    npg = min(DEC_PAGES_PER_STEP, n_pages)
    n_heads = d // LANES
    row = pl.BlockSpec((1, 1, d), lambda b, s, pt: (b, 0, 0))
    head_rows = pl.BlockSpec((1, n_heads, LANES), lambda b, s, pt: (b, 0, 0))
    const = lambda shape: pl.BlockSpec(shape, lambda b, s, pt: (0,) * len(shape))
    kt_pages = _page_specs(n_pages, npg, (d, PAGE_SIZE), first_page)
    v_pages = _page_specs(n_pages, npg, (PAGE_SIZE * n_heads, LANES), first_page)
    spread = np.repeat(np.eye(PAGE_SIZE), n_heads, axis=1)
    out = pl.pallas_call(
        functools.partial(_df_decode_kernel, lam_init=lam_init),
        grid_spec=pltpu.PrefetchScalarGridSpec(
            num_scalar_prefetch=1,
            grid=(db, n_pages // npg),
            in_specs=[const(lam_vec.shape), const((1, LANES)), const((DEC_HEADS, PAGE_SIZE)),
                      const((DEC_HEADS, LANES)), const(spread.shape), row, row, head_rows] + kt_pages + v_pages,
            out_specs=head_rows,
            scratch_shapes=[pltpu.VMEM((DEC_HEADS, LANES), F32), pltpu.VMEM((DEC_HEADS, 1), F32),
                            pltpu.VMEM((DEC_HEADS, 1), F32)]),
        out_shape=jax.ShapeDtypeStruct((db, n_heads, LANES), BF16),
        compiler_params=_params("parallel", "arbitrary"),
        name="df_decode",
    )(page_table, lam_vec, subln_g.reshape(1, LANES), dec_bias, new_bias, jnp.asarray(spread, BF16),
      q.reshape(db, 1, d), k_new.reshape(db, 1, d), v_new.reshape(db, n_heads, LANES),
      *([cache_kt] * npg), *([cache_v] * npg))
    return out.reshape(db, d)


def kernel(x_prompt, x_sample, cache_sb_k, cache_sb_v, cache_df_k, cache_df_v, page_table,
           w_in, w_out, ln_g, ln_b, df_lambda, df_subln_g, rel_bias, w_up, w_down):
    batch, seq, d = x_prompt.shape
    db = x_sample.shape[0]
    depth = w_in.shape[0]
    n_sb_heads = d // HEAD_DIM
    n_df_heads = d // (2 * HEAD_DIM)
    alpha = (2 * depth) ** 0.25
    q_scale = ATT_SCALE * LOG2E
    t = min(ATT_BLOCK, seq)
    assert x_sample.shape[1] == 1 and d // HEAD_DIM == DEC_HEADS
    assert cache_sb_k.shape[2] == PAGE_SIZE and seq % t == 0 and t % MXU_DIM == 0

    xp = x_prompt.reshape(batch * seq, d)
    xs = x_sample.reshape(db, d)
    w_in_b, w_out_b = w_in.astype(BF16), w_out.astype(BF16)
    w_up_b, w_down_b = w_up.astype(BF16), w_down.astype(BF16)
    n_pool = cache_sb_k.shape[1]
    feature_major = lambda cache: jnp.moveaxis(cache, 2, -1).reshape(-1, d, PAGE_SIZE)
    df_v_pages = cache_df_v.reshape(-1, PAGE_SIZE * n_df_heads, 2 * HEAD_DIM)
    token_major = lambda xt, feat: jnp.moveaxis(xt.reshape((batch,) + feat + (seq,)), -1, 1)

    sb_kp, sb_vp, sb_ks, sb_vs = [], [], [], []
    df_kp, df_vp, df_ks, df_vs = [], [], [], []
    for i in range(depth):
        j = i // 2
        qp, kpt, kptb, vp, vpb = _qkv_proj_fm(xp, w_in_b[i], q_scale, batch, v_feature_major=(i % 2 == 0))
        qs, ks_, vs_ = _qkv_proj(xs, w_in_b[i], q_scale)
        if i % 2 == 0:
            op = _sb_prompt_attention(qp, kptb, vpb, batch, seq)
            os_ = _sb_decode_attention(qs, feature_major(cache_sb_k), feature_major(cache_sb_v),
                                       page_table, j * n_pool)
            sb_kp.append(token_major(kpt, (n_sb_heads, HEAD_DIM)))
            sb_vp.append(token_major(vp, (n_sb_heads, HEAD_DIM)))
            sb_ks.append(ks_.reshape(db, 1, n_sb_heads, HEAD_DIM))
            sb_vs.append(vs_.reshape(db, 1, n_sb_heads, HEAD_DIM))
        else:
            lam_init = 0.8 - 0.6 * math.exp(-0.3 * i)
            tab, dec_bias, new_bias = _bias_tables(rel_bias, t)
            dec_bias = jnp.tile(dec_bias.reshape(n_df_heads, PAGE_SIZE), (2, 1))
            new_bias = jnp.tile(new_bias.reshape(n_df_heads, LANES), (2, 1))
            op = _df_prompt_attention(qp, kptb, vpb, tab, df_lambda[j], df_subln_g[j], batch, seq, lam_init)
            os_ = _df_decode_attention(qs, ks_, vs_, feature_major(cache_df_k), df_v_pages, page_table, j * n_pool,
                                       dec_bias, new_bias, df_lambda[j], df_subln_g[j], lam_init)
            df_kp.append(token_major(kpt, (n_df_heads, 2, HEAD_DIM)))
            df_vp.append(vp.reshape(batch, seq, n_df_heads, 2 * HEAD_DIM))
            df_ks.append(ks_.reshape(db, 1, n_df_heads, 2, HEAD_DIM))
            df_vs.append(vs_.reshape(db, 1, n_df_heads, 2 * HEAD_DIM))
        xp = _outproj_ln(op, xp, w_out_b[i], ln_g[i, 0], ln_b[i, 0], alpha)
        xs = _outproj_ln(os_, xs, w_out_b[i], ln_g[i, 0], ln_b[i, 0], alpha)
        xp = _mlp_ln(xp, w_up_b[i], w_down_b[i], ln_g[i, 1], ln_b[i, 1], alpha)
        xs = _mlp_ln(xs, w_up_b[i], w_down_b[i], ln_g[i, 1], ln_b[i, 1], alpha)
    return (xp.reshape(batch, seq, d), xs.reshape(db, 1, d),
            jnp.stack(sb_kp), jnp.stack(sb_vp), jnp.stack(sb_ks), jnp.stack(sb_vs),
            jnp.stack(df_kp), jnp.stack(df_vp), jnp.stack(df_ks), jnp.stack(df_vs))
```
